```python
import jax, jax.numpy as jnp
from jax import lax
import numpy as np

D_MODEL = 1024
BATCH = 8
SEQ = 4096
DEPTH = 1
DEC_BATCH = 32
DEC_SEQ = 8
PAST_LEN = 16384
PAGE_SIZE = 128

HEAD_DIM = 64
MIX_W = D_MODEL
RWKV_W = MIX_W // 2
ATT_W = MIX_W - RWKV_W
H_R = RWKV_W // HEAD_DIM
H_A = ATT_W // HEAD_DIM
D_DECAY = 64
D_ICLR = 64
D_GATE = 128
RWKV_COLS = 3 * RWKV_W + D_DECAY + D_ICLR + D_GATE
D_IN = RWKV_COLS + 3 * ATT_W
DIL_PATTERN = ((128, 1), (512, 4), (2048, 16))
WINDOW_MAX = max(w for w, _ in DIL_PATTERN)
ROPE_THETA = 10000.0
D_FF = 2816
CONV_W = 3
NORM_EPS = 1e-6
LNX_EPS = 64e-5
NEG_INF = -1e30

kernel_name = 'hymba_rwkv7_dilated_convffn_step'


def _rms(x, g):
    xf = x.astype(jnp.float32)
    y = xf * lax.rsqrt(jnp.mean(xf * xf, axis=-1, keepdims=True) + NORM_EPS)
    return (y * g.astype(jnp.float32)).astype(x.dtype)


def _rope(t, pos):
    half = t.shape[-1] // 2
    inv = ROPE_THETA ** (-jnp.arange(half, dtype=jnp.float32) / half)
    ang = pos.astype(jnp.float32)[:, None] * inv[None, :]
    cos = jnp.cos(ang)[None, :, None, :]
    sin = jnp.sin(ang)[None, :, None, :]
    tf = t.astype(jnp.float32)
    t1, t2 = tf[..., :half], tf[..., half:]
    return jnp.concatenate([t1 * cos - t2 * sin, t1 * sin + t2 * cos], axis=-1).astype(t.dtype)


def _dilated_branch_prompt(q, k, v, dilation, n_steps):
    B, L, H, Dh = q.shape
    qb = n_steps
    seg = dilation * qb
    Lp = -(-L // seg) * seg
    pad = Lp - L
    M = Lp // dilation
    nb = M // qb

    def to_sub(t):
        t = jnp.pad(t, ((0, 0), (pad, 0), (0, 0), (0, 0)))
        t = t.reshape(B, M, dilation, H, Dh).transpose(0, 2, 1, 3, 4)
        return t.reshape(B, dilation, nb, qb, H, Dh)

    def with_prev(t):
        prev = jnp.pad(t, ((0, 0), (0, 0), (1, 0), (0, 0), (0, 0), (0, 0)))[:, :, :-1]
        return jnp.concatenate([prev, t], axis=3)

    qs = to_sub(q)
    kb = with_prev(to_sub(k))
    vb = with_prev(to_sub(v))
    s = jnp.einsum('brnqhd,brnkhd->brnhqk', qs, kb).astype(jnp.float32) * (Dh ** -0.5)

    qi = jnp.arange(qb)[:, None]
    ki = jnp.arange(2 * qb)[None, :]
    dist = qi + qb - ki
    band = (dist >= 0) & (dist <= n_steps)
    key_sub = jnp.arange(nb)[:, None] * qb + jnp.arange(2 * qb)[None, :] - qb
    key_pos = key_sub[None] * dilation + jnp.arange(dilation)[:, None, None] - pad
    mask = band[None, None] & (key_pos >= 0)[:, :, None, :]
    mask = mask[None, :, :, None]

    s = jnp.where(mask, s, NEG_INF)
    m = jnp.max(s, axis=-1)
    p = jnp.where(mask, jnp.exp(s - m[..., None]), 0.0)
    l = jnp.sum(p, axis=-1)
    o = jnp.einsum('brnhqk,brnkhd->brnqhd', p, vb.astype(jnp.float32))

    def from_sub(t):
        rest = t.shape[4:]
        t = t.reshape((B, dilation, M) + rest).swapaxes(1, 2).reshape((B, Lp) + rest)
        return t[:, pad:]

    return from_sub(o), from_sub(m.swapaxes(3, 4)), from_sub(l.swapaxes(3, 4))


def _dilated_branch_sample(q, kc, vc, dilation, n_steps, offset):
    T = q.shape[1]
    j = jnp.arange(n_steps + 1)
    idx = offset + jnp.arange(T)[:, None] - j[None, :] * dilation
    valid = idx >= 0
    idx = jnp.maximum(idx, 0)
    kg = kc[:, idx]
    vg = vc[:, idx]
    s = jnp.einsum('bthd,btjhd->bthj', q, kg).astype(jnp.float32) * (q.shape[-1] ** -0.5)
    mask = valid[None, :, None, :]
    s = jnp.where(mask, s, NEG_INF)
    m = jnp.max(s, axis=-1)
    p = jnp.where(mask, jnp.exp(s - m[..., None]), 0.0)
    l = jnp.sum(p, axis=-1)
    o = jnp.einsum('bthj,btjhd->bthd', p, vg.astype(jnp.float32))
    return o, m, l


def _merge_branches(branches):
    o = jnp.stack([b[0] for b in branches])
    m = jnp.stack([b[1] for b in branches])
    l = jnp.stack([b[2] for b in branches])
    wt = jnp.exp(m - jnp.max(m, axis=0))
    num = jnp.sum(wt[..., None] * o, axis=0)
    den = jnp.sum(wt * l, axis=0)
    return num / den[..., None]


def _wkv_step(S, inp):
    r, w, k, v, a, b = inp
    sa = jnp.einsum('bhij,bhj->bhi', S, a)
    S = S * w[:, :, None, :] + sa[..., None] * b[:, :, None, :] + v[..., None] * k[:, :, None, :]
    return S, jnp.einsum('bhij,bhj->bhi', S, r)


def _rwkv7(r, k, v, wl, al, gl, S0, w0, w_decay_up, a0, w_iclr_up, w_gate_up,
           k_k, k_a, r_k, lnx_w, lnx_b):
    B, T, _ = r.shape
    f32 = jnp.float32
    r, k, v = r.astype(f32), k.astype(f32), v.astype(f32)
    w = -jax.nn.softplus(-(w0 + jnp.tanh(wl.astype(f32)) @ w_decay_up)) - 0.5
    decay = jnp.exp(-jnp.exp(w))
    a = jax.nn.sigmoid(a0 + al.astype(f32) @ w_iclr_up)
    g = jax.nn.sigmoid(gl.astype(f32)) @ w_gate_up
    heads = lambda t: t.reshape(B, T, H_R, HEAD_DIM)
    kk = heads(k * k_k)
    kk = kk / jnp.maximum(jnp.sqrt(jnp.sum(kk * kk, axis=-1, keepdims=True)), 1e-12)
    k = k * (1.0 + (a - 1.0) * k_a)
    rh, kh, vh, wh = heads(r), heads(k), heads(v), heads(decay)
    aa = -kk
    bb = kk * heads(a)
    tm = lambda t: jnp.swapaxes(t, 0, 1)
    S_fin, y = lax.scan(_wkv_step, S0.astype(f32),
                        (tm(rh), tm(wh), tm(kh), tm(vh), tm(aa), tm(bb)))
    y = tm(y)
    mu = jnp.mean(y, axis=-1, keepdims=True)
    var = jnp.mean(jnp.square(y - mu), axis=-1, keepdims=True)
    y = ((y - mu) * lax.rsqrt(var + LNX_EPS)).reshape(B, T, RWKV_W) * lnx_w + lnx_b
    bonus = jnp.sum(rh * kh * r_k, axis=-1, keepdims=True) * vh
    y = (y + bonus.reshape(B, T, RWKV_W)) * g
    return y, S_fin


def _layer(x, pos, h_prev, wkv0, k_buf, v_buf, conv_prev,
           norm_mix_pre, norm_mix_post, norm_ffn_pre, norm_ffn_post, w_in, mu_shift,
           w0, w_decay_up, a0, w_iclr_up, w_gate_up, k_k, k_a, r_k, lnx_w, lnx_b,
           w_out, w_ffn_up, ffn_conv_w, ffn_conv_b, w_ffn_down):
    B, T, _ = x.shape
    h = _rms(x, norm_mix_pre)
    proj = jnp.concatenate([h_prev[:, None].astype(h.dtype), h], axis=1) @ w_in
    cur = proj[:, 1:]
    prv = proj[:, :-1, :RWKV_COLS]
    rw = cur[..., :RWKV_COLS]
    mixed = rw + (prv - rw) * mu_shift
    r, k, v, wl, al, gl = jnp.split(mixed, [RWKV_W, 2 * RWKV_W, 3 * RWKV_W,
                                            3 * RWKV_W + D_DECAY,
                                            3 * RWKV_W + D_DECAY + D_ICLR], axis=-1)
    rwkv_out, wkv_new = _rwkv7(r, k, v, wl, al, gl, wkv0, w0, w_decay_up, a0, w_iclr_up,
                               w_gate_up, k_k, k_a, r_k, lnx_w, lnx_b)

    qa, ka, va = [t.reshape(B, T, H_A, HEAD_DIM)
                  for t in jnp.split(cur[..., RWKV_COLS:], 3, axis=-1)]
    qa = _rope(qa, pos)
    ka = _rope(ka, pos)
    if k_buf is None:
        branches = [_dilated_branch_prompt(qa, ka, va, d, w // d) for w, d in DIL_PATTERN]
        keep = min(WINDOW_MAX, T)
        k_new, v_new = ka[:, -keep:], va[:, -keep:]
    else:
        W = k_buf.shape[1]
        kc = jnp.concatenate([k_buf.astype(ka.dtype), ka], axis=1)
        vc = jnp.concatenate([v_buf.astype(va.dtype), va], axis=1)
        branches = [_dilated_branch_sample(qa, kc, vc, d, w // d, W) for w, d in DIL_PATTERN]
        k_new, v_new = kc[:, -W:], vc[:, -W:]
    att_out = _merge_branches(branches).reshape(B, T, ATT_W)

    mix = jnp.concatenate([rwkv_out.astype(x.dtype), att_out.astype(x.dtype)], axis=-1) @ w_out
    x = x + _rms(mix, norm_mix_post).astype(x.dtype)

    h2 = _rms(x, norm_ffn_pre)
    u = h2 @ w_ffn_up
    ue = jnp.concatenate([conv_prev.astype(u.dtype), u], axis=1)
    c = sum(ue[:, i:i + T] * ffn_conv_w[i] for i in range(CONV_W)) + ffn_conv_b
    gate, up = jnp.split(c, 2, axis=-1)
    f = (jax.nn.silu(gate) * up) @ w_ffn_down
    x = x + _rms(f, norm_ffn_post).astype(x.dtype)
    return x, h[:, -1], wkv_new, k_new, v_new, ue[:, -(CONV_W - 1):]


def setup_inputs(seed: int = 0) -> dict:
    key = jax.random.key(seed)
    ks = jax.random.split(key, 32)
    nrm = jax.random.normal
    W_BUF = min(WINDOW_MAX, PAST_LEN)
    L = DEPTH
    return {
        'x_prompt': nrm(ks[0], (BATCH, SEQ, D_MODEL), jnp.float32),
        'x_sample': nrm(ks[1], (DEC_BATCH, DEC_SEQ, D_MODEL), jnp.float32),
        'state_rwkv_shift': nrm(ks[2], (L, DEC_BATCH, D_MODEL), jnp.float32),
        'state_rwkv_wkv': nrm(ks[3], (L, DEC_BATCH, H_R, HEAD_DIM, HEAD_DIM), jnp.float32),
        'cache_att_k': nrm(ks[4], (L, DEC_BATCH, W_BUF, H_A, HEAD_DIM), jnp.float32),
        'cache_att_v': nrm(ks[5], (L, DEC_BATCH, W_BUF, H_A, HEAD_DIM), jnp.float32),
        'state_ffn_conv': nrm(ks[6], (L, DEC_BATCH, CONV_W - 1, 2 * D_FF), jnp.float32),
        'norm_mix_pre': 1.0 + 0.05 * nrm(ks[7], (L, D_MODEL), jnp.float32),
        'norm_mix_post': 1.0 + 0.05 * nrm(ks[8], (L, D_MODEL), jnp.float32),
        'norm_ffn_pre': 1.0 + 0.05 * nrm(ks[9], (L, D_MODEL), jnp.float32),
        'norm_ffn_post': 1.0 + 0.05 * nrm(ks[10], (L, D_MODEL), jnp.float32),
        'w_in': nrm(ks[11], (L, D_MODEL, D_IN), jnp.float32) * D_MODEL ** -0.5,
        'mu_shift': jax.random.uniform(ks[12], (L, RWKV_COLS), jnp.float32),
        'w0': -0.5 - 3.5 * jax.random.uniform(ks[13], (L, RWKV_W), jnp.float32),
        'w_decay_up': nrm(ks[14], (L, D_DECAY, RWKV_W), jnp.float32) * 0.1 * D_DECAY ** -0.5,
        'a0': 0.1 * nrm(ks[15], (L, RWKV_W), jnp.float32),
        'w_iclr_up': nrm(ks[16], (L, D_ICLR, RWKV_W), jnp.float32) * D_ICLR ** -0.5,
        'w_gate_up': nrm(ks[17], (L, D_GATE, RWKV_W), jnp.float32) * D_GATE ** -0.5,
        'k_k': 0.85 + 0.05 * nrm(ks[18], (L, RWKV_W), jnp.float32),
        'k_a': 1.0 + 0.05 * nrm(ks[19], (L, RWKV_W), jnp.float32),
        'r_k': 0.1 * nrm(ks[20], (L, H_R, HEAD_DIM), jnp.float32),
        'lnx_w': 1.0 + 0.05 * nrm(ks[21], (L, RWKV_W), jnp.float32),
        'lnx_b': 0.01 * nrm(ks[22], (L, RWKV_W), jnp.float32),
        'w_out': nrm(ks[23], (L, MIX_W, D_MODEL), jnp.float32) * MIX_W ** -0.5,
        'w_ffn_up': nrm(ks[24], (L, D_MODEL, 2 * D_FF), jnp.float32) * D_MODEL ** -0.5,
        'ffn_conv_w': nrm(ks[25], (L, CONV_W, 2 * D_FF), jnp.float32) * CONV_W ** -0.5,
        'ffn_conv_b': 0.01 * nrm(ks[26], (L, 2 * D_FF), jnp.float32),
        'w_ffn_down': nrm(ks[27], (L, D_FF, D_MODEL), jnp.float32) * D_FF ** -0.5,
    }


def reference(x_prompt, x_sample, state_rwkv_shift, state_rwkv_wkv, cache_att_k, cache_att_v,
              state_ffn_conv, norm_mix_pre, norm_mix_post, norm_ffn_pre, norm_ffn_post, w_in,
              mu_shift, w0, w_decay_up, a0, w_iclr_up, w_gate_up, k_k, k_a, r_k, lnx_w, lnx_b,
              w_out, w_ffn_up, ffn_conv_w, ffn_conv_b, w_ffn_down):
    Bp, Tp, _ = x_prompt.shape
    Ts = x_sample.shape[1]
    pos_p = jnp.arange(Tp, dtype=jnp.int32)
    pos_s = PAST_LEN + jnp.arange(Ts, dtype=jnp.int32)
    yp, ys = x_prompt, x_sample
    p_shift, p_wkv, p_k, p_v, p_conv = [], [], [], [], []
    s_shift, s_wkv, s_k, s_v, s_conv = [], [], [], [], []
    for l in range(DEPTH):
        wts = (norm_mix_pre[l], norm_mix_post[l], norm_ffn_pre[l], norm_ffn_post[l], w_in[l],
               mu_shift[l], w0[l], w_decay_up[l], a0[l], w_iclr_up[l], w_gate_up[l], k_k[l],
               k_a[l], r_k[l], lnx_w[l], lnx_b[l], w_out[l], w_ffn_up[l], ffn_conv_w[l],
               ffn_conv_b[l], w_ffn_down[l])
        yp, a1, a2, a3, a4, a5 = _layer(
            yp, pos_p, jnp.zeros((Bp, D_MODEL), x_prompt.dtype),
            jnp.zeros((Bp, H_R, HEAD_DIM, HEAD_DIM), jnp.float32), None, None,
            jnp.zeros((Bp, CONV_W - 1, 2 * D_FF), x_prompt.dtype), *wts)
        p_shift.append(a1); p_wkv.append(a2); p_k.append(a3); p_v.append(a4); p_conv.append(a5)
        ys, b1, b2, b3, b4, b5 = _layer(
            ys, pos_s, state_rwkv_shift[l], state_rwkv_wkv[l], cache_att_k[l], cache_att_v[l],
            state_ffn_conv[l], *wts)
        s_shift.append(b1); s_wkv.append(b2); s_k.append(b3); s_v.append(b4); s_conv.append(b5)
    return (yp, ys,
            jnp.stack(p_shift), jnp.stack(p_wkv), jnp.stack(p_k), jnp.stack(p_v), jnp.stack(p_conv),
            jnp.stack(s_shift), jnp.stack(s_wkv), jnp.stack(s_k), jnp.stack(s_v), jnp.stack(s_conv))
```

```python
import functools

import jax
import jax.numpy as jnp
from jax import lax
from jax.experimental import pallas as pl
from jax.experimental.pallas import tpu as pltpu

F32 = jnp.float32
BF16 = jnp.bfloat16

D_MODEL = 1024
HEAD_DIM = 64
RWKV_W = 512
ATT_W = 512
N_HEADS = 8
N_PAIRS = N_HEADS // 2
D_DECAY = 64
D_ICLR = 64
D_GATE = 128
RWKV_COLS = 3 * RWKV_W + D_DECAY + D_ICLR + D_GATE
D_IN = RWKV_COLS + 3 * ATT_W
DILATIONS = (1, 4, 16)
N_STEPS = 128
ROPE_THETA = 10000.0
D_FF = 2816
CONV_W = 3
NORM_EPS = 1e-6
LNX_EPS = 64e-5
NEG_INF = -1e30
PAST_LEN = 16384

LANES = 128
SUBLANES = 8
VMEM_LIMIT = 56 * 1024 * 1024

NN = (((1,), (0,)), ((), ()))
NT = (((1,), (1,)), ((), ()))
TN = (((0,), (0,)), ((), ()))


def _mm(a, b, dims=NN):
    return lax.dot_general(a, b, dims, preferred_element_type=F32)


def _split(x):
    hi = x.astype(BF16)
    lo = (x - hi.astype(F32)).astype(BF16)
    return hi, lo


def _mm1(a, b, dims=NN):
    return _mm(a.astype(BF16), b.astype(BF16), dims)


def _mm3(a, b, dims=NN):
    ah, al = _split(a)
    bh, bl = _split(b)
    return _mm(ah, bh, dims) + (_mm(ah, bl, dims) + _mm(al, bh, dims))


def _mm_split_lhs(a, b_bf16):
    ah, al = _split(a)
    return _mm(ah, b_bf16) + _mm(al, b_bf16)


def _rms(x, g):
    return x * lax.rsqrt(jnp.mean(x * x, axis=-1, keepdims=True) + NORM_EPS) * g


def _sigmoid(x):
    return 1.0 / (1.0 + jnp.exp(-x))


def _params(sem):
    return pltpu.CompilerParams(dimension_semantics=sem, vmem_limit_bytes=VMEM_LIMIT)


def _const_spec(shape):
    nd = len(shape)
    return pl.BlockSpec(shape, lambda *_: (0,) * nd, pipeline_mode=pl.Buffered(1))


def _in_proj_kernel(x_ref, g_ref, w_ref, cos_ref, sin_ref, rw_ref, q_ref, k_ref, v_ref):
    hb = _rms(x_ref[...], g_ref[...]).astype(BF16)
    rw_ref[...] = _mm(hb, w_ref[:, :RWKV_COLS])
    cos = jnp.concatenate([cos_ref[...]] * (ATT_W // LANES), axis=1)
    sin = jnp.concatenate([sin_ref[...]] * (ATT_W // LANES), axis=1)
    lane = lax.broadcasted_iota(jnp.int32, (1, ATT_W), 1)
    first_half = (lane % HEAD_DIM) < HEAD_DIM // 2

    def rope(t):
        partner = jnp.where(first_half, pltpu.roll(t, ATT_W - HEAD_DIM // 2, 1),
                            pltpu.roll(t, HEAD_DIM // 2, 1))
        return t * cos + partner * sin

    q0 = RWKV_COLS
    q_ref[...] = rope(_mm(hb, w_ref[:, q0:q0 + ATT_W]))
    k_ref[...] = rope(_mm(hb, w_ref[:, q0 + ATT_W:q0 + 2 * ATT_W]))
    v_ref[...] = _mm(hb, w_ref[:, q0 + 2 * ATT_W:q0 + 3 * ATT_W])


def _in_proj(x2d, g, w_bf, cos_t, sin_t, tm, pos_tiles):
    n = x2d.shape[0]
    row = lambda i: (i, 0)
    pos = lambda i: (i % pos_tiles, 0)
    return pl.pallas_call(
        _in_proj_kernel,
        grid=(n // tm,),
        in_specs=[pl.BlockSpec((tm, D_MODEL), row), _const_spec((1, D_MODEL)),
                  _const_spec((D_MODEL, D_IN)),
                  pl.BlockSpec((tm, LANES), pos), pl.BlockSpec((tm, LANES), pos)],
        out_specs=[pl.BlockSpec((tm, RWKV_COLS), row), pl.BlockSpec((tm, ATT_W), row),
                   pl.BlockSpec((tm, ATT_W), row), pl.BlockSpec((tm, ATT_W), row)],
        out_shape=[jax.ShapeDtypeStruct((n, RWKV_COLS), F32)] + [jax.ShapeDtypeStruct((n, ATT_W), F32)] * 3,
        compiler_params=_params(("parallel",)),
        name="in_proj",
    )(x2d, g, w_bf, cos_t, sin_t)


def _shift_state_kernel(x_ref, g_ref, h_ref, w_ref, hl_ref, ps_ref):
    hl_ref[...] = _rms(x_ref[...], g_ref[...])
    ps_ref[...] = _mm(h_ref[...].astype(BF16), w_ref[...])


def _shift_state(x_last, g, h_prev, w_rw_bf):
    b = x_last.shape[0]
    return pl.pallas_call(
        _shift_state_kernel,
        out_shape=[jax.ShapeDtypeStruct((b, D_MODEL), F32), jax.ShapeDtypeStruct((b, RWKV_COLS), F32)],
        compiler_params=_params(None),
        name="shift_state",
    )(x_last, g, h_prev, w_rw_bf)


def _rwkv_kernel(p_ref, ps_ref, s0_ref, mu_ref, w0_ref, a0_ref, lora_hi_ref, lora_lo_ref,
                 wg_hi_ref, wg_lo_ref, kk_ref, ka_ref, rk_ref, lnw_ref, lnb_ref,
                 y_ref, sfin_ref, s_scr, prev_scr, *, C, n_chunks):
    ci = pl.program_id(1)
    C2 = 2 * C

    @pl.when(ci == 0)
    def _():
        prev_scr[...] = ps_ref[...]
        z = jnp.zeros((HEAD_DIM, HEAD_DIM), F32)
        for p in range(N_PAIRS):
            top = jnp.concatenate([s0_ref[2 * p], z], axis=1)
            bot = jnp.concatenate([z, s0_ref[2 * p + 1]], axis=1)
            s_scr[p] = jnp.concatenate([top, bot], axis=0)

    P = p_ref[...]
    row_c = lax.broadcasted_iota(jnp.int32, (C, 1), 0)
    shifted = jnp.where(row_c == 0, prev_scr[...], pltpu.roll(P, 1, 0))
    prev_scr[...] = P[C - 1:C, :]
    mixed = P + (shifted - P) * mu_ref[...]

    r = mixed[:, 0:RWKV_W]
    k = mixed[:, RWKV_W:2 * RWKV_W]
    v = mixed[:, 2 * RWKV_W:3 * RWKV_W]
    xl = mixed[:, 3 * RWKV_W:3 * RWKV_W + LANES]
    gl = mixed[:, 3 * RWKV_W + LANES:RWKV_COLS]

    lane = lax.broadcasted_iota(jnp.int32, (1, LANES), 1)
    lane_lo = lane < HEAD_DIM
    th, tl = _split(jnp.where(lane_lo, jnp.tanh(xl), xl))
    lora = _mm(th, lora_hi_ref[...]) + (_mm(th, lora_lo_ref[...]) + _mm(tl, lora_hi_ref[...]))
    w_pre = w0_ref[...] + lora[:, :RWKV_W]
    a_pre = a0_ref[...] + lora[:, RWKV_W:]
    z = -w_pre
    softplus = jnp.maximum(z, 0.0) + jnp.log1p(jnp.exp(-jnp.abs(z)))
    logd = -jnp.exp(-softplus - 0.5)
    a_lr = _sigmoid(a_pre)
    sgh, sgl = _split(_sigmoid(gl))
    gate = _mm(sgh, wg_hi_ref[...]) + (_mm(sgh, wg_lo_ref[...]) + _mm(sgl, wg_hi_ref[...]))

    ri = lax.broadcasted_iota(jnp.int32, (LANES, LANES), 0)
    cj = lax.broadcasted_iota(jnp.int32, (LANES, LANES), 1)
    seg_ones = jnp.where((ri < HEAD_DIM) == (cj < HEAD_DIM), 1.0, 0.0).astype(BF16)

    def seg_sum(x):
        return jnp.concatenate(
            [_mm_split_lhs(x[:, LANES * p:LANES * (p + 1)], seg_ones) for p in range(N_PAIRS)], axis=1)

    kkr = k * kk_ref[...]
    kkn = kkr / jnp.maximum(jnp.sqrt(seg_sum(kkr * kkr)), 1e-12)
    kmod = k * (1.0 + (a_lr - 1.0) * ka_ref[...])
    bvec = kkn * a_lr
    avec = -kkn

    c = logd
    s = 1
    while s < C:
        c = c + jnp.where(row_c >= s, pltpu.roll(c, s, 0), 0.0)
        s *= 2
    mid = C // 2 - 1
    c_mid = c[mid:mid + 1, :]
    cp = c - c_mid
    e_pos = jnp.exp(cp)
    e_neg = jnp.exp(-cp)
    e_prev = jnp.exp(cp - logd)
    e_mid = jnp.exp(c_mid)
    e_end = jnp.exp(c[C - 1:C, :])
    e_endp = jnp.exp(cp[C - 1:C, :])
    a_t = avec * e_prev
    r_t = r * e_pos
    b_t = bvec * e_neg
    k_t = kmod * e_neg
    a_abs = a_t * e_mid
    r_abs = r_t * e_mid
    b_bar = b_t * e_endp
    k_bar = k_t * e_endp
    bonus_in = r * kmod * rk_ref[...]

    def stack(x):
        return jnp.concatenate([jnp.where(lane_lo, x, 0.0), jnp.where(lane_lo, 0.0, x)], axis=0)

    ri2 = lax.broadcasted_iota(jnp.int32, (C2, C2), 0)
    cj2 = lax.broadcasted_iota(jnp.int32, (C2, C2), 1)
    strict = cj2 < ri2
    incl = cj2 <= ri2
    eye = jnp.where(ri2 == cj2, 1.0, 0.0)
    n_sq = C.bit_length() - 2

    for p in range(N_PAIRS):
        sl = slice(LANES * p, LANES * (p + 1))
        a1, r1, b1, k1 = stack(a_t[:, sl]), stack(r_t[:, sl]), stack(b_t[:, sl]), stack(k_t[:, sl])
        v_st = stack(v[:, sl])
        g_ab = jnp.where(strict, _mm3(a1, b1, NT), 0.0)
        g_ak = jnp.where(strict, _mm3(a1, k1, NT), 0.0)
        g_rb = jnp.where(incl, _mm3(r1, b1, NT), 0.0)
        g_rk = jnp.where(incl, _mm3(r1, k1, NT), 0.0)

        lp = g_ab
        t_inv = eye + lp
        for _ in range(n_sq):
            lp = _mm3(lp, lp)
            t_inv = t_inv + _mm3(t_inv, lp)

        w2 = _mm3(t_inv, _mm3(g_ak, v_st))
        w1 = _mm3(t_inv, stack(a_abs[:, sl]))
        s_old = s_scr[p]
        u = _mm3(w1, s_old, NT) + w2
        y_st = _mm3(stack(r_abs[:, sl]), s_old, NT) + _mm3(g_rb, u) + _mm3(g_rk, v_st)
        s_new = (s_old * e_end[:, sl] + _mm3(u, stack(b_bar[:, sl]), TN)
                 + _mm3(v_st, stack(k_bar[:, sl]), TN))
        s_scr[p] = s_new
        y = y_st[:C] + y_st[C:]

        def seg_mean(x):
            return _mm_split_lhs(x, seg_ones) * (1.0 / HEAD_DIM)

        dev = y - seg_mean(y)
        var = seg_mean(dev * dev)
        yn = dev * lax.rsqrt(var + LNX_EPS) * lnw_ref[:, sl] + lnb_ref[:, sl]
        bonus = _mm_split_lhs(bonus_in[:, sl], seg_ones) * v[:, sl]
        y_ref[:, sl] = (yn + bonus) * gate[:, sl]

    @pl.when(ci == n_chunks - 1)
    def _():
        for p in range(N_PAIRS):
            s_fin = s_scr[p]
            sfin_ref[2 * p] = s_fin[:HEAD_DIM, :HEAD_DIM]
            sfin_ref[2 * p + 1] = s_fin[HEAD_DIM:, HEAD_DIM:]


def _rwkv(proj_rw, pstart, s0, wts, C):
    b, t, _ = proj_rw.shape
    n_chunks = t // C
    vec = lambda n: _const_spec((1, n))
    kern = functools.partial(_rwkv_kernel, C=C, n_chunks=n_chunks)
    return pl.pallas_call(
        kern,
        grid=(b, n_chunks),
        in_specs=[pl.BlockSpec((None, C, RWKV_COLS), lambda i, j: (i, j, 0)),
                  pl.BlockSpec((None, 1, RWKV_COLS), lambda i, j: (i, 0, 0)),
                  pl.BlockSpec((None, N_HEADS, HEAD_DIM, HEAD_DIM), lambda i, j: (i, 0, 0, 0)),
                  vec(RWKV_COLS), vec(RWKV_W), vec(RWKV_W),
                  _const_spec((LANES, 2 * RWKV_W)), _const_spec((LANES, 2 * RWKV_W)),
                  _const_spec((D_GATE, RWKV_W)), _const_spec((D_GATE, RWKV_W)),
                  vec(RWKV_W), vec(RWKV_W), vec(RWKV_W), vec(RWKV_W), vec(RWKV_W)],
        out_specs=[pl.BlockSpec((None, C, RWKV_W), lambda i, j: (i, j, 0)),
                   pl.BlockSpec((None, N_HEADS, HEAD_DIM, HEAD_DIM), lambda i, j: (i, 0, 0, 0))],
        out_shape=[jax.ShapeDtypeStruct((b, t, RWKV_W), F32),
                   jax.ShapeDtypeStruct((b, N_HEADS, HEAD_DIM, HEAD_DIM), F32)],
        scratch_shapes=[pltpu.VMEM((N_PAIRS, LANES, LANES), F32), pltpu.VMEM((1, RWKV_COLS), F32)],
        compiler_params=_params(("parallel", "arbitrary")),
        name="rwkv",
    )(proj_rw, pstart, s0, *wts)


def _attn_prompt_kernel(q_ref, k_ref, v_ref, o_ref, acc_scr, m_scr, l_scr, *, T):
    QB = N_STEPS
    lane = lax.broadcasted_iota(jnp.int32, (1, LANES), 1)
    lane_lo = lane < HEAD_DIM
    ri = lax.broadcasted_iota(jnp.int32, (2 * QB, QB), 0)
    cj = lax.broadcasted_iota(jnp.int32, (2 * QB, QB), 1)
    diff = cj - (ri % QB)
    mask_cur = diff <= 0

    def halves(x):
        return jnp.where(lane_lo, x[:QB], x[QB:])

    for bi, d in enumerate(DILATIONS):
        per_res = T // (QB * d)

        def rows(start, d=d):
            if d == 1:
                return pl.ds(pl.multiple_of(start, QB), QB)
            return pl.ds(start, QB, stride=d)

        def body(it, carry, bi=bi, d=d, per_res=per_res, rows=rows):
            res = it // per_res
            n = it % per_res
            start = res + d * QB * n
            startp = res + d * QB * jnp.maximum(n - 1, 0)
            q = q_ref[rows(start), :] * (HEAD_DIM ** -0.5)
            q_st = jnp.concatenate([jnp.where(lane_lo, q, 0.0), jnp.where(lane_lo, 0.0, q)],
                                   axis=0).astype(BF16)
            kp = k_ref[rows(startp), :].astype(BF16)
            kc = k_ref[rows(start), :].astype(BF16)
            vp = v_ref[rows(startp), :].astype(BF16)
            vc = v_ref[rows(start), :].astype(BF16)
            mask_prev = diff >= jnp.where(n > 0, 0, QB)
            sp = jnp.where(mask_prev, _mm(q_st, kp, NT), NEG_INF)
            sc = jnp.where(mask_cur, _mm(q_st, kc, NT), NEG_INF)
            m = jnp.maximum(jnp.max(sp, axis=1, keepdims=True), jnp.max(sc, axis=1, keepdims=True))
            pp = jnp.where(mask_prev, jnp.exp(sp - m), 0.0)
            pc = jnp.where(mask_cur, jnp.exp(sc - m), 0.0)
            l = jnp.sum(pp, axis=1, keepdims=True) + jnp.sum(pc, axis=1, keepdims=True)
            o = halves(_mm(pp.astype(BF16), vp) + _mm(pc.astype(BF16), vc))
            m_b = halves(jnp.broadcast_to(m, (2 * QB, LANES)))
            l_b = halves(jnp.broadcast_to(l, (2 * QB, LANES)))
            if bi == 0:
                acc_scr[rows(start), :] = o
                m_scr[rows(start), :] = m_b
                l_scr[rows(start), :] = l_b
            else:
                m_old = m_scr[rows(start), :]
                m_new = jnp.maximum(m_old, m_b)
                w_old = jnp.exp(m_old - m_new)
                w_new = jnp.exp(m_b - m_new)
                acc = acc_scr[rows(start), :] * w_old + o * w_new
                den = l_scr[rows(start), :] * w_old + l_b * w_new
                if bi < len(DILATIONS) - 1:
                    acc_scr[rows(start), :] = acc
                    m_scr[rows(start), :] = m_new
                    l_scr[rows(start), :] = den
                else:
                    o_ref[rows(start), :] = acc / den
            return carry

        lax.fori_loop(0, T // QB, body, 0)


def _attn_prompt(q, k, v):
    b, t, _ = q.shape
    assert t % (N_STEPS * max(DILATIONS)) == 0
    spec = pl.BlockSpec((None, t, LANES), lambda i, p: (i, 0, p))
    return pl.pallas_call(
        functools.partial(_attn_prompt_kernel, T=t),
        grid=(b, N_PAIRS),
        in_specs=[spec, spec, spec],
        out_specs=spec,
        out_shape=jax.ShapeDtypeStruct((b, t, ATT_W), F32),
        scratch_shapes=[pltpu.VMEM((t, LANES), F32)] * 3,
        compiler_params=_params(("parallel", "parallel")),
        name="attn_prompt",
    )(q, k, v)


def _attn_sample_kernel(q_ref, kn_ref, vn_ref, kc_ref, vc_ref, cntc_ref, cntn_ref,
                        o_ref, ko_ref, vo_ref, *, ts, w_buf):
    rows = N_HEADS * ts
    ri = lax.broadcasted_iota(jnp.int32, (rows, ATT_W), 0)
    cj = lax.broadcasted_iota(jnp.int32, (rows, ATT_W), 1)
    own = (ri // ts) == (cj // HEAD_DIM)
    q_all = jnp.concatenate([q_ref[...] * (HEAD_DIM ** -0.5)] * N_HEADS, axis=0)
    q_st = jnp.where(own, q_all, 0.0).astype(BF16)
    kc = kc_ref[...]
    vc = vc_ref[...]
    kn = kn_ref[...]
    vn = vn_ref[...]
    cnt_c = cntc_ref[...]
    cnt_n = cntn_ref[...]
    s_c = jnp.where(cnt_c > 0.0, _mm(q_st, kc.astype(BF16), NT), NEG_INF)
    s_n = jnp.where(cnt_n > 0.0, _mm(q_st, kn.astype(BF16), NT), NEG_INF)
    m = jnp.maximum(jnp.max(s_c, axis=1, keepdims=True), jnp.max(s_n, axis=1, keepdims=True))
    p_c = cnt_c * jnp.exp(s_c - m)
    p_n = cnt_n * jnp.exp(s_n - m)
    den = jnp.sum(p_c, axis=1, keepdims=True) + jnp.sum(p_n, axis=1, keepdims=True)
    o_st = (_mm(p_c.astype(BF16), vc.astype(BF16)) + _mm(p_n.astype(BF16), vn.astype(BF16))) / den
    o_st = jnp.where(own, o_st, 0.0)
    out = o_st[0:ts]
    for h in range(1, N_HEADS):
        out = out + o_st[h * ts:(h + 1) * ts]
    o_ref[...] = out
    ko_ref[0:w_buf - ts, :] = kc[ts:w_buf]
    ko_ref[w_buf - ts:w_buf, :] = kn
    vo_ref[0:w_buf - ts, :] = vc[ts:w_buf]
    vo_ref[w_buf - ts:w_buf, :] = vn


def _branch_counts(ts, w_buf):
    delta = (w_buf + jnp.arange(ts)[:, None]) - jnp.arange(w_buf + ts)[None, :]
    cnt = jnp.zeros(delta.shape, F32)
    for d in DILATIONS:
        cnt = cnt + ((delta >= 0) & (delta % d == 0) & (delta // d <= N_STEPS)).astype(F32)
    cnt = jnp.tile(cnt, (N_HEADS, 1))
    return cnt[:, :w_buf], cnt[:, w_buf:]


def _attn_sample(q, kn, vn, kcache, vcache):
    b, ts, _ = q.shape
    w_buf = kcache.shape[1]
    assert w_buf >= N_STEPS * max(DILATIONS) and ts % SUBLANES == 0
    cnt_c, cnt_n = _branch_counts(ts, w_buf)
    new = pl.BlockSpec((None, ts, ATT_W), lambda i: (i, 0, 0))
    cache = pl.BlockSpec((None, w_buf, ATT_W), lambda i: (i, 0, 0))
    return pl.pallas_call(
        functools.partial(_attn_sample_kernel, ts=ts, w_buf=w_buf),
        grid=(b,),
        in_specs=[new, new, new, cache, cache,
                  _const_spec((N_HEADS * ts, w_buf)), _const_spec((N_HEADS * ts, ts))],
        out_specs=[new, cache, cache],
        out_shape=[jax.ShapeDtypeStruct((b, ts, ATT_W), F32),
                   jax.ShapeDtypeStruct((b, w_buf, ATT_W), F32),
                   jax.ShapeDtypeStruct((b, w_buf, ATT_W), F32)],
        compiler_params=_params(("parallel",)),
        name="attn_sample",
    )(q, kn, vn, kcache, vcache, cnt_c, cnt_n)


def _out_proj_kernel(x_ref, a_ref, b_ref, w_ref, g_ref, y_ref):
    mix = _mm(a_ref[...].astype(BF16), w_ref[:RWKV_W, :]) + _mm(b_ref[...].astype(BF16), w_ref[RWKV_W:, :])
    y_ref[...] = x_ref[...] + _rms(mix, g_ref[...])


def _out_proj(x2d, rw, att, w_bf, g, tm):
    n = x2d.shape[0]
    row = lambda i: (i, 0)
    return pl.pallas_call(
        _out_proj_kernel,
        grid=(n // tm,),
        in_specs=[pl.BlockSpec((tm, D_MODEL), row), pl.BlockSpec((tm, RWKV_W), row),
                  pl.BlockSpec((tm, ATT_W), row), _const_spec((D_MODEL, D_MODEL)),
                  _const_spec((1, D_MODEL))],
        out_specs=pl.BlockSpec((tm, D_MODEL), row),
        out_shape=jax.ShapeDtypeStruct((n, D_MODEL), F32),
        compiler_params=_params(("parallel",)),
        name="out_proj",
    )(x2d, rw, att, w_bf, g)


FFN_TF = 256
N_FCHUNK = D_FF // FFN_TF


def _ffn_kernel(*refs, tm, tiles_per_seq, seq_len, has_state):
    if has_state:
        (x_ref, gpre_ref, gpost_ref, wg_ref, wu_ref, cwg_ref, cwu_ref, cbg_ref, cbu_ref, wd_ref,
         p1g_ref, p1u_ref, p2g_ref, p2u_ref, y_ref, ug_ref, uu_ref, h_scr, acc_scr) = refs
    else:
        (x_ref, gpre_ref, gpost_ref, wg_ref, wu_ref, cwg_ref, cwu_ref, cbg_ref, cbu_ref, wd_ref,
         y_ref, ug_ref, uu_ref, h_scr, acc_scr, carry_scr) = refs
    i = pl.program_id(0)
    j = pl.program_id(1)

    @pl.when(j == 0)
    def _():
        h_scr[...] = _rms(x_ref[...], gpre_ref[...]).astype(BF16)
        acc_scr[...] = jnp.zeros_like(acc_scr)

    hb = h_scr[...]
    ug = _mm(hb, wg_ref[...])
    uu = _mm(hb, wu_ref[...])
    row = lax.broadcasted_iota(jnp.int32, (tm, 1), 0)

    if has_state:
        t_in = row % seq_len
        prev = ((p1g_ref[...], p2g_ref[...]), (p1u_ref[...], p2u_ref[...]))
        ug_ref[...] = ug
        uu_ref[...] = uu
    else:
        t_in = row
        @pl.when((i % tiles_per_seq) == 0)
        def _():
            carry_scr[j] = jnp.zeros((2, SUBLANES, FFN_TF), F32)

        prev = []
        for idx in range(2):
            tail = carry_scr[j, idx]
            p1 = tail[SUBLANES - 1:SUBLANES]
            p2 = jnp.where(row == 0, tail[SUBLANES - 2:SUBLANES - 1], p1)
            prev.append((p1, p2))
        carry_scr[j, 0] = ug[tm - SUBLANES:]
        carry_scr[j, 1] = uu[tm - SUBLANES:]
        ug_ref[...] = ug[tm - SUBLANES:]
        uu_ref[...] = uu[tm - SUBLANES:]

    def conv(u, p, cw_ref, cb_ref):
        u1 = jnp.where(t_in >= 1, pltpu.roll(u, 1, 0), p[0])
        u2 = jnp.where(t_in >= 2, pltpu.roll(u, 2, 0), p[1])
        return u2 * cw_ref[0:1, :] + u1 * cw_ref[1:2, :] + u * cw_ref[2:3, :] + cb_ref[...]

    cg = conv(ug, prev[0], cwg_ref, cbg_ref)
    cu = conv(uu, prev[1], cwu_ref, cbu_ref)
    act = (cg * _sigmoid(cg) * cu).astype(BF16)
    acc_scr[...] += _mm(act, wd_ref[...])

    @pl.when(j == N_FCHUNK - 1)
    def _():
        y_ref[...] = x_ref[...] + _rms(acc_scr[...], gpost_ref[...])


def _ffn(x2d, g_pre, g_post, wup_bf, conv_w, conv_b, wdown_bf, tm, seq_len, state_rows=None):
    n = x2d.shape[0]
    has_state = state_rows is not None
    tiles_per_seq = max(seq_len // tm, 1)
    row = lambda i, j: (i, 0)
    gate_col = lambda i, j: (0, j)
    up_col = lambda i, j: (0, N_FCHUNK + j)
    in_specs = [pl.BlockSpec((tm, D_MODEL), row),
                pl.BlockSpec((1, D_MODEL), lambda i, j: (0, 0)),
                pl.BlockSpec((1, D_MODEL), lambda i, j: (0, 0)),
                pl.BlockSpec((D_MODEL, FFN_TF), gate_col), pl.BlockSpec((D_MODEL, FFN_TF), up_col),
                pl.BlockSpec((CONV_W, FFN_TF), gate_col), pl.BlockSpec((CONV_W, FFN_TF), up_col),
                pl.BlockSpec((1, FFN_TF), gate_col), pl.BlockSpec((1, FFN_TF), up_col),
                pl.BlockSpec((FFN_TF, D_MODEL), lambda i, j: (j, 0))]
    args = [x2d, g_pre, g_post, wup_bf, wup_bf, conv_w, conv_w, conv_b, conv_b, wdown_bf]
    scratch = [pltpu.VMEM((tm, D_MODEL), BF16), pltpu.VMEM((tm, D_MODEL), F32)]
    if has_state:
        p1, p2 = state_rows
        tile_g = lambda i, j: (i, j)
        tile_u = lambda i, j: (i, N_FCHUNK + j)
        in_specs += [pl.BlockSpec((tm, FFN_TF), tile_g), pl.BlockSpec((tm, FFN_TF), tile_u),
                     pl.BlockSpec((tm, FFN_TF), tile_g), pl.BlockSpec((tm, FFN_TF), tile_u)]
        args += [p1, p1, p2, p2]
        u_spec = pl.BlockSpec((tm, FFN_TF), tile_g)
        u_shape = jax.ShapeDtypeStruct((n, D_FF), F32)
    else:
        assert seq_len % tm == 0
        scratch.append(pltpu.VMEM((N_FCHUNK, 2, SUBLANES, FFN_TF), F32))
        u_spec = pl.BlockSpec((None, SUBLANES, FFN_TF), lambda i, j: (i, 0, j))
        u_shape = jax.ShapeDtypeStruct((n // tm, SUBLANES, D_FF), F32)
    kern = functools.partial(_ffn_kernel, tm=tm, tiles_per_seq=tiles_per_seq, seq_len=seq_len,
                             has_state=has_state)
    return pl.pallas_call(
        kern,
        grid=(n // tm, N_FCHUNK),
        in_specs=in_specs,
        out_specs=[pl.BlockSpec((tm, D_MODEL), row), u_spec, u_spec],
        out_shape=[jax.ShapeDtypeStruct((n, D_MODEL), F32), u_shape, u_shape],
        scratch_shapes=scratch,
        compiler_params=_params(("arbitrary", "arbitrary")),
        name="ffn",
    )(*args)


def _rope_tables(pos):
    half = HEAD_DIM // 2
    inv = ROPE_THETA ** (-jnp.arange(half, dtype=F32) / half)
    ang = pos.astype(F32)[:, None] * inv[None, :]
    cos, sin = jnp.cos(ang), jnp.sin(ang)
    cos_t = jnp.tile(jnp.concatenate([cos, cos], axis=1), (1, LANES // HEAD_DIM))
    sin_t = jnp.tile(jnp.concatenate([-sin, sin], axis=1), (1, LANES // HEAD_DIM))
    return cos_t, sin_t


def _layer(x, pos, h_prev, wkv0, k_buf, v_buf, conv_prev, w, tm, chunk):
    b, t, _ = x.shape
    n = b * t
    x2d = x.reshape(n, D_MODEL)
    cos_t, sin_t = _rope_tables(pos)
    if t < tm:
        cos_t = jnp.tile(cos_t, (tm // t, 1))
        sin_t = jnp.tile(sin_t, (tm // t, 1))
        pos_tiles = 1
    else:
        pos_tiles = t // tm
    rw, q, k, v = _in_proj(x2d, w["norm_mix_pre"], w["w_in"], cos_t, sin_t, tm, pos_tiles)
    h_last, pstart = _shift_state(x[:, -1], w["norm_mix_pre"], h_prev, w["w_in_rw"])

    rwkv_out, wkv_new = _rwkv(rw.reshape(b, t, RWKV_COLS), pstart.reshape(b, 1, RWKV_COLS), wkv0,
                              w["rwkv"], chunk)
    q3, k3, v3 = (a.reshape(b, t, ATT_W) for a in (q, k, v))
    if k_buf is None:
        att = _attn_prompt(q3, k3, v3)
        keep = min(N_STEPS * max(DILATIONS), t)
        k_new, v_new = k3[:, -keep:], v3[:, -keep:]
    else:
        w_buf = k_buf.shape[1]
        att, k_new, v_new = _attn_sample(q3, k3, v3, k_buf.reshape(b, w_buf, ATT_W),
                                         v_buf.reshape(b, w_buf, ATT_W))
    x1 = _out_proj(x2d, rwkv_out.reshape(n, RWKV_W), att.reshape(n, ATT_W), w["w_out"],
                   w["norm_mix_post"], tm)

    if conv_prev is None:
        y, ug, uu = _ffn(x1, w["norm_ffn_pre"], w["norm_ffn_post"], w["w_ffn_up"], w["ffn_conv_w"],
                         w["ffn_conv_b"], w["w_ffn_down"], tm, t)
        tails = jnp.concatenate([ug, uu], axis=-1).reshape(b, t // tm, SUBLANES, 2 * D_FF)
        conv_new = tails[:, -1, SUBLANES - (CONV_W - 1):]
    else:
        zeros = jnp.zeros((b, t, 2 * D_FF), F32)
        p1 = zeros.at[:, 0].set(conv_prev[:, 1]).reshape(n, 2 * D_FF)
        p2 = zeros.at[:, 0:2].set(conv_prev).reshape(n, 2 * D_FF)
        y, ug, uu = _ffn(x1, w["norm_ffn_pre"], w["norm_ffn_post"], w["w_ffn_up"], w["ffn_conv_w"],
                         w["ffn_conv_b"], w["w_ffn_down"], tm, t, state_rows=(p1, p2))
        conv_new = jnp.concatenate([ug, uu], axis=-1).reshape(b, t, 2 * D_FF)[:, t - (CONV_W - 1):]
    heads = lambda a: a.reshape(b, a.shape[1], N_HEADS, HEAD_DIM)
    return y.reshape(b, t, D_MODEL), h_last, wkv_new, heads(k_new), heads(v_new), conv_new


def _prep_weights(norm_mix_pre, norm_mix_post, norm_ffn_pre, norm_ffn_post, w_in, mu_shift, w0,
                  w_decay_up, a0, w_iclr_up, w_gate_up, k_k, k_a, r_k, lnx_w, lnx_b, w_out,
                  w_ffn_up, ffn_conv_w, ffn_conv_b, w_ffn_down):
    vec = lambda a: a.reshape(1, -1)
    zero = jnp.zeros((D_DECAY, RWKV_W), F32)
    lora = jnp.concatenate([jnp.concatenate([w_decay_up, zero], axis=1),
                            jnp.concatenate([zero, w_iclr_up], axis=1)], axis=0)
    lora_hi, lora_lo = _split(lora)
    wg_hi, wg_lo = _split(w_gate_up)
    w_in_bf = w_in.astype(BF16)
    return {
        "norm_mix_pre": vec(norm_mix_pre), "norm_mix_post": vec(norm_mix_post),
        "norm_ffn_pre": vec(norm_ffn_pre), "norm_ffn_post": vec(norm_ffn_post),
        "w_in": w_in_bf, "w_in_rw": w_in_bf[:, :RWKV_COLS],
        "rwkv": (vec(mu_shift), vec(w0), vec(a0), lora_hi, lora_lo, wg_hi, wg_lo,
                 vec(k_k), vec(k_a), vec(r_k), vec(lnx_w), vec(lnx_b)),
        "w_out": w_out.astype(BF16), "w_ffn_up": w_ffn_up.astype(BF16),
        "ffn_conv_w": ffn_conv_w, "ffn_conv_b": vec(ffn_conv_b), "w_ffn_down": w_ffn_down.astype(BF16),
    }


PROMPT_TM = 512
PROMPT_CHUNK = 64


def kernel(x_prompt, x_sample, state_rwkv_shift, state_rwkv_wkv, cache_att_k, cache_att_v, state_ffn_conv, norm_mix_pre, norm_mix_post, norm_ffn_pre, norm_ffn_post, w_in, mu_shift, w0, w_decay_up, a0, w_iclr_up, w_gate_up, k_k, k_a, r_k, lnx_w, lnx_b, w_out, w_ffn_up, ffn_conv_w, ffn_conv_b, w_ffn_down):
    bp, tp, _ = x_prompt.shape
    bs, ts, _ = x_sample.shape
    depth = norm_mix_pre.shape[0]
    pos_p = jnp.arange(tp, dtype=jnp.int32)
    pos_s = PAST_LEN + jnp.arange(ts, dtype=jnp.int32)
    yp, ys = x_prompt, x_sample
    outs_p = [[] for _ in range(5)]
    outs_s = [[] for _ in range(5)]
    for l in range(depth):
        w = _prep_weights(norm_mix_pre[l], norm_mix_post[l], norm_ffn_pre[l], norm_ffn_post[l], w_in[l],
                          mu_shift[l], w0[l], w_decay_up[l], a0[l], w_iclr_up[l], w_gate_up[l], k_k[l],
                          k_a[l], r_k[l], lnx_w[l], lnx_b[l], w_out[l], w_ffn_up[l], ffn_conv_w[l],
                          ffn_conv_b[l], w_ffn_down[l])
        yp, *state_p = _layer(yp, pos_p, jnp.zeros((bp, D_MODEL), F32),
                              jnp.zeros((bp, N_HEADS, HEAD_DIM, HEAD_DIM), F32), None, None, None,
                              w, min(PROMPT_TM, tp), min(PROMPT_CHUNK, tp))
        ys, *state_s = _layer(ys, pos_s, state_rwkv_shift[l], state_rwkv_wkv[l], cache_att_k[l],
                              cache_att_v[l], state_ffn_conv[l], w, bs * ts, ts)
        for acc, val in zip(outs_p, state_p):
            acc.append(val)
        for acc, val in zip(outs_s, state_s):
            acc.append(val)
    return (yp, ys, *(jnp.stack(a) for a in outs_p), *(jnp.stack(a) for a in outs_s))
```

```python
import functools

import jax
import jax.numpy as jnp
from jax import lax
from jax.experimental import pallas as pl
from jax.experimental.pallas import tpu as pltpu

F32 = jnp.float32
BF16 = jnp.bfloat16

D_MODEL = 1024
HEAD_DIM = 64
RWKV_W = 512
ATT_W = 512
N_HEADS = 8
N_PAIRS = N_HEADS // 2
D_DECAY = 64
D_ICLR = 64
D_GATE = 128
RWKV_COLS = 3 * RWKV_W + D_DECAY + D_ICLR + D_GATE
D_IN = RWKV_COLS + 3 * ATT_W
DILATIONS = (1, 4, 16)
N_STEPS = 128
ROPE_THETA = 10000.0
D_FF = 2816
CONV_W = 3
NORM_EPS = 1e-6
LNX_EPS = 64e-5
NEG_INF = -1e30
PAST_LEN = 16384

LANES = 128
SUBLANES = 8
VMEM_LIMIT = 56 * 1024 * 1024

NN = (((1,), (0,)), ((), ()))
NT = (((1,), (1,)), ((), ()))
TN = (((0,), (0,)), ((), ()))


def _mm(a, b, dims=NN):
    return lax.dot_general(a, b, dims, preferred_element_type=F32)


def _split(x):
    hi = x.astype(BF16)
    lo = (x - hi.astype(F32)).astype(BF16)
    return hi, lo


def _mm1(a, b, dims=NN):
    return _mm(a.astype(BF16), b.astype(BF16), dims)


def _mm3(a, b, dims=NN):
    ah, al = _split(a)
    bh, bl = _split(b)
    return _mm(ah, bh, dims) + (_mm(ah, bl, dims) + _mm(al, bh, dims))


_mm_gram = _mm1
_mm_inv = _mm1
_mm_apply = _mm1
_mm_state = _mm1


def _mm_split_lhs(a, b_bf16):
    ah, al = _split(a)
    return _mm(ah, b_bf16) + _mm(al, b_bf16)


def _rms(x, g):
    return x * lax.rsqrt(jnp.mean(x * x, axis=-1, keepdims=True) + NORM_EPS) * g


def _sigmoid(x):
    return 1.0 / (1.0 + jnp.exp(-x))


def _params(sem):
    return pltpu.CompilerParams(dimension_semantics=sem, vmem_limit_bytes=VMEM_LIMIT)


def _const_spec(shape):
    nd = len(shape)
    return pl.BlockSpec(shape, lambda *_: (0,) * nd, pipeline_mode=pl.Buffered(1))


def _in_proj_kernel(x_ref, g_ref, w_ref, cos_ref, sin_ref, rw_ref, q_ref, k_ref, v_ref):
    hb = _rms(x_ref[...], g_ref[...]).astype(BF16)
    rw_ref[...] = _mm(hb, w_ref[:, :RWKV_COLS])
    cos = jnp.concatenate([cos_ref[...]] * (ATT_W // LANES), axis=1)
    sin = jnp.concatenate([sin_ref[...]] * (ATT_W // LANES), axis=1)
    lane = lax.broadcasted_iota(jnp.int32, (1, ATT_W), 1)
    first_half = (lane % HEAD_DIM) < HEAD_DIM // 2

    def rope(t):
        partner = jnp.where(first_half, pltpu.roll(t, ATT_W - HEAD_DIM // 2, 1),
                            pltpu.roll(t, HEAD_DIM // 2, 1))
        return t * cos + partner * sin

    q0 = RWKV_COLS
    q_ref[...] = rope(_mm(hb, w_ref[:, q0:q0 + ATT_W]))
    k_ref[...] = rope(_mm(hb, w_ref[:, q0 + ATT_W:q0 + 2 * ATT_W]))
    v_ref[...] = _mm(hb, w_ref[:, q0 + 2 * ATT_W:q0 + 3 * ATT_W])


def _in_proj(x2d, g, w_bf, cos_t, sin_t, tm, pos_tiles):
    n = x2d.shape[0]
    row = lambda i: (i, 0)
    pos = lambda i: (i % pos_tiles, 0)
    return pl.pallas_call(
        _in_proj_kernel,
        grid=(n // tm,),
        in_specs=[pl.BlockSpec((tm, D_MODEL), row), _const_spec((1, D_MODEL)),
                  _const_spec((D_MODEL, D_IN)),
                  pl.BlockSpec((tm, LANES), pos), pl.BlockSpec((tm, LANES), pos)],
        out_specs=[pl.BlockSpec((tm, RWKV_COLS), row), pl.BlockSpec((tm, ATT_W), row),
                   pl.BlockSpec((tm, ATT_W), row), pl.BlockSpec((tm, ATT_W), row)],
        out_shape=[jax.ShapeDtypeStruct((n, RWKV_COLS), F32)] + [jax.ShapeDtypeStruct((n, ATT_W), F32)] * 3,
        compiler_params=_params(("parallel",)),
        name="in_proj",
    )(x2d, g, w_bf, cos_t, sin_t)


def _shift_state_kernel(x_ref, g_ref, h_ref, w_ref, hl_ref, ps_ref):
    hl_ref[...] = _rms(x_ref[...], g_ref[...])
    ps_ref[...] = _mm(h_ref[...].astype(BF16), w_ref[...])


def _shift_state(x_last, g, h_prev, w_rw_bf):
    b = x_last.shape[0]
    return pl.pallas_call(
        _shift_state_kernel,
        out_shape=[jax.ShapeDtypeStruct((b, D_MODEL), F32), jax.ShapeDtypeStruct((b, RWKV_COLS), F32)],
        compiler_params=_params(None),
        name="shift_state",
    )(x_last, g, h_prev, w_rw_bf)


def _rwkv_kernel(p_ref, ps_ref, s0_ref, mu_ref, w0_ref, a0_ref, lora_hi_ref, lora_lo_ref,
                 wg_hi_ref, wg_lo_ref, kk_ref, ka_ref, rk_ref, lnw_ref, lnb_ref,
                 y_ref, sfin_ref, s_scr, prev_scr, *, C, n_chunks):
    ci = pl.program_id(1)
    C2 = 2 * C

    @pl.when(ci == 0)
    def _():
        prev_scr[...] = ps_ref[...]
        z = jnp.zeros((HEAD_DIM, HEAD_DIM), F32)
        for p in range(N_PAIRS):
            top = jnp.concatenate([s0_ref[2 * p], z], axis=1)
            bot = jnp.concatenate([z, s0_ref[2 * p + 1]], axis=1)
            s_scr[p] = jnp.concatenate([top, bot], axis=0)

    P = p_ref[...]
    row_c = lax.broadcasted_iota(jnp.int32, (C, 1), 0)
    shifted = jnp.where(row_c == 0, prev_scr[...], pltpu.roll(P, 1, 0))
    prev_scr[...] = P[C - 1:C, :]
    mixed = P + (shifted - P) * mu_ref[...]

    r = mixed[:, 0:RWKV_W]
    k = mixed[:, RWKV_W:2 * RWKV_W]
    v = mixed[:, 2 * RWKV_W:3 * RWKV_W]
    xl = mixed[:, 3 * RWKV_W:3 * RWKV_W + LANES]
    gl = mixed[:, 3 * RWKV_W + LANES:RWKV_COLS]

    lane = lax.broadcasted_iota(jnp.int32, (1, LANES), 1)
    lane_lo = lane < HEAD_DIM
    th, tl = _split(jnp.where(lane_lo, jnp.tanh(xl), xl))
    lora = _mm(th, lora_hi_ref[...]) + (_mm(th, lora_lo_ref[...]) + _mm(tl, lora_hi_ref[...]))
    w_pre = w0_ref[...] + lora[:, :RWKV_W]
    a_pre = a0_ref[...] + lora[:, RWKV_W:]
    z = -w_pre
    softplus = jnp.maximum(z, 0.0) + jnp.log1p(jnp.exp(-jnp.abs(z)))
    logd = -jnp.exp(-softplus - 0.5)
    a_lr = _sigmoid(a_pre)
    sgh, sgl = _split(_sigmoid(gl))
    gate = _mm(sgh, wg_hi_ref[...]) + (_mm(sgh, wg_lo_ref[...]) + _mm(sgl, wg_hi_ref[...]))

    ri = lax.broadcasted_iota(jnp.int32, (LANES, LANES), 0)
    cj = lax.broadcasted_iota(jnp.int32, (LANES, LANES), 1)
    seg_ones = jnp.where((ri < HEAD_DIM) == (cj < HEAD_DIM), 1.0, 0.0).astype(BF16)

    def seg_sum(x):
        return jnp.concatenate(
            [_mm_split_lhs(x[:, LANES * p:LANES * (p + 1)], seg_ones) for p in range(N_PAIRS)], axis=1)

    kkr = k * kk_ref[...]
    kkn = kkr / jnp.maximum(jnp.sqrt(seg_sum(kkr * kkr)), 1e-12)
    kmod = k * (1.0 + (a_lr - 1.0) * ka_ref[...])
    bvec = kkn * a_lr
    avec = -kkn

    c = logd
    s = 1
    while s < C:
        c = c + jnp.where(row_c >= s, pltpu.roll(c, s, 0), 0.0)
        s *= 2
    mid = C // 2 - 1
    c_mid = c[mid:mid + 1, :]
    cp = c - c_mid
    e_pos = jnp.exp(cp)
    e_neg = jnp.exp(-cp)
    e_prev = jnp.exp(cp - logd)
    e_mid = jnp.exp(c_mid)
    e_end = jnp.exp(c[C - 1:C, :])
    e_endp = jnp.exp(cp[C - 1:C, :])
    a_t = avec * e_prev
    r_t = r * e_pos
    b_t = bvec * e_neg
    k_t = kmod * e_neg
    bonus_in = r * kmod * rk_ref[...]

    def stack(x):
        return jnp.concatenate([jnp.where(lane_lo, x, 0.0), jnp.where(lane_lo, 0.0, x)], axis=0)

    ri2 = lax.broadcasted_iota(jnp.int32, (C2, C2), 0)
    cj2 = lax.broadcasted_iota(jnp.int32, (C2, C2), 1)
    strict = cj2 < ri2
    incl = cj2 <= ri2
    n_sq = C.bit_length() - 2
    fused = C2 % LANES == 0
    pairs = range(N_PAIRS)
    sls = [slice(LANES * p, LANES * (p + 1)) for p in pairs]

    ar = [jnp.concatenate([stack(a_t[:, sl]), stack(r_t[:, sl])], axis=0) for sl in sls]
    bk = [jnp.concatenate([stack(b_t[:, sl]), stack(k_t[:, sl])], axis=0) for sl in sls]
    v_st = [stack(v[:, sl]) for sl in sls]
    if fused:
        gram = [_mm_gram(ar[p], bk[p], NT) for p in pairs]
        g_ab = [jnp.where(strict, g[:C2, :C2], 0.0) for g in gram]
        g_ak = [jnp.where(strict, g[:C2, C2:], 0.0) for g in gram]
        g_rb = [jnp.where(incl, g[C2:, :C2], 0.0) for g in gram]
        g_rk = [jnp.where(incl, g[C2:, C2:], 0.0) for g in gram]
    else:
        g_ab = [jnp.where(strict, _mm_gram(ar[p][:C2], bk[p][:C2], NT), 0.0) for p in pairs]
        g_ak = [jnp.where(strict, _mm_gram(ar[p][:C2], bk[p][C2:], NT), 0.0) for p in pairs]
        g_rb = [jnp.where(incl, _mm_gram(ar[p][C2:], bk[p][:C2], NT), 0.0) for p in pairs]
        g_rk = [jnp.where(incl, _mm_gram(ar[p][C2:], bk[p][C2:], NT), 0.0) for p in pairs]

    z2 = [_mm_apply(g_ak[p], v_st[p]) for p in pairs]
    x = [jnp.concatenate([ar[p][:C2] * e_mid[:, sls[p]], z2[p]], axis=1) for p in pairs]
    lp = g_ab
    x = [x[p] + _mm_apply(lp[p], x[p]) for p in pairs]
    for _ in range(n_sq):
        lp = [_mm_inv(lp[p], lp[p]) for p in pairs]
        x = [x[p] + _mm_apply(lp[p], x[p]) for p in pairs]

    s_old = [s_scr[p] for p in pairs]
    wr = [jnp.concatenate([x[p][:, :LANES], ar[p][C2:] * e_mid[:, sls[p]]], axis=0) for p in pairs]
    ws = [_mm_state(wr[p], s_old[p], NT) for p in pairs]
    u = [ws[p][:C2] + x[p][:, LANES:] for p in pairs]
    uv = [jnp.concatenate([u[p], v_st[p]], axis=0) for p in pairs]
    if fused:
        y_st = [ws[p][C2:] + _mm_state(jnp.concatenate([g_rb[p], g_rk[p]], axis=1), uv[p]) for p in pairs]
    else:
        y_st = [ws[p][C2:] + _mm_state(g_rb[p], u[p]) + _mm_state(g_rk[p], v_st[p]) for p in pairs]
    for p in pairs:
        s_scr[p] = s_old[p] * e_end[:, sls[p]] + _mm_state(uv[p], bk[p] * e_endp[:, sls[p]], TN)

    def seg_mean(x):
        return _mm_split_lhs(x, seg_ones) * (1.0 / HEAD_DIM)

    y = [y_st[p][:C] + y_st[p][C:] for p in pairs]
    dev = [y[p] - seg_mean(y[p]) for p in pairs]
    var = [seg_mean(dev[p] * dev[p]) for p in pairs]
    for p in pairs:
        sl = sls[p]
        yn = dev[p] * lax.rsqrt(var[p] + LNX_EPS) * lnw_ref[:, sl] + lnb_ref[:, sl]
        bonus = _mm_split_lhs(bonus_in[:, sl], seg_ones) * v[:, sl]
        y_ref[:, sl] = (yn + bonus) * gate[:, sl]

    @pl.when(ci == n_chunks - 1)
    def _():
        for p in range(N_PAIRS):
            s_fin = s_scr[p]
            sfin_ref[2 * p] = s_fin[:HEAD_DIM, :HEAD_DIM]
            sfin_ref[2 * p + 1] = s_fin[HEAD_DIM:, HEAD_DIM:]


def _rwkv(proj_rw, pstart, s0, wts, C):
    b, t, _ = proj_rw.shape
    n_chunks = t // C
    vec = lambda n: _const_spec((1, n))
    kern = functools.partial(_rwkv_kernel, C=C, n_chunks=n_chunks)
    return pl.pallas_call(
        kern,
        grid=(b, n_chunks),
        in_specs=[pl.BlockSpec((None, C, RWKV_COLS), lambda i, j: (i, j, 0)),
                  pl.BlockSpec((None, 1, RWKV_COLS), lambda i, j: (i, 0, 0)),
                  pl.BlockSpec((None, N_HEADS, HEAD_DIM, HEAD_DIM), lambda i, j: (i, 0, 0, 0)),
                  vec(RWKV_COLS), vec(RWKV_W), vec(RWKV_W),
                  _const_spec((LANES, 2 * RWKV_W)), _const_spec((LANES, 2 * RWKV_W)),
                  _const_spec((D_GATE, RWKV_W)), _const_spec((D_GATE, RWKV_W)),
                  vec(RWKV_W), vec(RWKV_W), vec(RWKV_W), vec(RWKV_W), vec(RWKV_W)],
        out_specs=[pl.BlockSpec((None, C, RWKV_W), lambda i, j: (i, j, 0)),
                   pl.BlockSpec((None, N_HEADS, HEAD_DIM, HEAD_DIM), lambda i, j: (i, 0, 0, 0))],
        out_shape=[jax.ShapeDtypeStruct((b, t, RWKV_W), F32),
                   jax.ShapeDtypeStruct((b, N_HEADS, HEAD_DIM, HEAD_DIM), F32)],
        scratch_shapes=[pltpu.VMEM((N_PAIRS, LANES, LANES), F32), pltpu.VMEM((1, RWKV_COLS), F32)],
        compiler_params=_params(("parallel", "arbitrary")),
        name="rwkv",
    )(proj_rw, pstart, s0, *wts)


ATT_UNROLL = 4


def _attn_prompt_kernel(q_ref, k_ref, v_ref, bias_ref, o_ref, acc_scr, m_scr, l_scr, *, T):
    QB = N_STEPS
    lane = lax.broadcasted_iota(jnp.int32, (1, LANES), 1)
    lane_lo = lane < HEAD_DIM

    def halves(x):
        return jnp.where(lane_lo, x[:QB], x[QB:])

    order = sorted(DILATIONS, reverse=True)
    for bi, d in enumerate(order):
        per_res = T // (QB * d)
        ub = min(ATT_UNROLL, per_res)
        groups = ATT_UNROLL // ub
        gpr = per_res // ub

        def rows(start, d=d):
            if d == 1:
                return pl.ds(pl.multiple_of(start, QB), QB)
            return pl.ds(start, QB, stride=d)

        def body(it, carry, bi=bi, d=d, ub=ub, groups=groups, gpr=gpr, rows=rows):
            blocks = []
            for g in range(groups):
                gi = it * groups + g
                res = gi // gpr
                n0 = (gi % gpr) * ub
                starts = [res + d * QB * jnp.maximum(n0 - 1, 0)] + [res + d * QB * (n0 + u) for u in range(ub)]
                kb = [k_ref[rows(s), :].astype(BF16) for s in starts]
                vb = [v_ref[rows(s), :].astype(BF16) for s in starts]
                for u in range(ub):
                    bias = bias_ref[jnp.where(n0 == 0, 1, 0)] if u == 0 else bias_ref[0]
                    blocks.append((starts[u + 1], jnp.concatenate([kb[u], kb[u + 1]], axis=0),
                                   jnp.concatenate([vb[u], vb[u + 1]], axis=0), bias))
            q_st = []
            for start, _, _, _ in blocks:
                q = q_ref[rows(start), :] * (HEAD_DIM ** -0.5)
                q_st.append(jnp.concatenate([jnp.where(lane_lo, q, 0.0), jnp.where(lane_lo, 0.0, q)],
                                            axis=0).astype(BF16))
            s = [_mm(q_st[i], blk[1], NT) + blk[3] for i, blk in enumerate(blocks)]
            m = [jnp.max(x, axis=1, keepdims=True) for x in s]
            p = [jnp.exp(s[i] - m[i]) for i in range(len(blocks))]
            l = [jnp.sum(x, axis=1, keepdims=True) for x in p]
            o = [halves(_mm(p[i].astype(BF16), blk[2])) for i, blk in enumerate(blocks)]
            m_b = [halves(jnp.broadcast_to(x, (2 * QB, LANES))) for x in m]
            l_b = [halves(jnp.broadcast_to(x, (2 * QB, LANES))) for x in l]
            if bi == 0:
                for i, blk in enumerate(blocks):
                    acc_scr[rows(blk[0]), :] = o[i]
                    m_scr[rows(blk[0]), :] = m_b[i]
                    l_scr[rows(blk[0]), :] = l_b[i]
            else:
                m_old = [m_scr[rows(blk[0]), :] for blk in blocks]
                acc_old = [acc_scr[rows(blk[0]), :] for blk in blocks]
                l_old = [l_scr[rows(blk[0]), :] for blk in blocks]
                for i, blk in enumerate(blocks):
                    m_new = jnp.maximum(m_old[i], m_b[i])
                    w_old = jnp.exp(m_old[i] - m_new)
                    w_new = jnp.exp(m_b[i] - m_new)
                    acc = acc_old[i] * w_old + o[i] * w_new
                    den = l_old[i] * w_old + l_b[i] * w_new
                    if bi < len(order) - 1:
                        acc_scr[rows(blk[0]), :] = acc
                        m_scr[rows(blk[0]), :] = m_new
                        l_scr[rows(blk[0]), :] = den
                    else:
                        o_ref[rows(blk[0]), :] = acc / den
            return carry

        lax.fori_loop(0, d * gpr // groups, body, 0)


def _band_bias():
    qi = jnp.arange(2 * N_STEPS)[:, None] % N_STEPS
    kj = jnp.arange(2 * N_STEPS)[None, :]
    band = (kj >= qi) & (kj <= qi + N_STEPS)
    normal = jnp.where(band, 0.0, NEG_INF).astype(F32)
    first = jnp.where(band & (kj >= N_STEPS), 0.0, NEG_INF).astype(F32)
    return jnp.stack([normal, first])


def _attn_prompt(q, k, v):
    b, t, _ = q.shape
    assert t % (N_STEPS * max(DILATIONS)) == 0
    spec = pl.BlockSpec((None, t, LANES), lambda i, p: (i, 0, p))
    return pl.pallas_call(
        functools.partial(_attn_prompt_kernel, T=t),
        grid=(b, N_PAIRS),
        in_specs=[spec, spec, spec, _const_spec((2, 2 * N_STEPS, 2 * N_STEPS))],
        out_specs=spec,
        out_shape=jax.ShapeDtypeStruct((b, t, ATT_W), F32),
        scratch_shapes=[pltpu.VMEM((t, LANES), F32)] * 3,
        compiler_params=_params(("parallel", "parallel")),
        name="attn_prompt",
    )(q, k, v, _band_bias())


def _attn_sample_kernel(q_ref, kn_ref, vn_ref, kc_ref, vc_ref, cntc_ref, cntn_ref,
                        o_ref, ko_ref, vo_ref, *, ts, w_buf):
    rows = N_HEADS * ts
    ri = lax.broadcasted_iota(jnp.int32, (rows, ATT_W), 0)
    cj = lax.broadcasted_iota(jnp.int32, (rows, ATT_W), 1)
    own = (ri // ts) == (cj // HEAD_DIM)
    q_all = jnp.concatenate([q_ref[...] * (HEAD_DIM ** -0.5)] * N_HEADS, axis=0)
    q_st = jnp.where(own, q_all, 0.0).astype(BF16)
    kc = kc_ref[...]
    vc = vc_ref[...]
    kn = kn_ref[...]
    vn = vn_ref[...]
    cnt_c = cntc_ref[...]
    cnt_n = cntn_ref[...]
    s_c = jnp.where(cnt_c > 0.0, _mm(q_st, kc.astype(BF16), NT), NEG_INF)
    s_n = jnp.where(cnt_n > 0.0, _mm(q_st, kn.astype(BF16), NT), NEG_INF)
    m = jnp.maximum(jnp.max(s_c, axis=1, keepdims=True), jnp.max(s_n, axis=1, keepdims=True))
    p_c = cnt_c * jnp.exp(s_c - m)
    p_n = cnt_n * jnp.exp(s_n - m)
    den = jnp.sum(p_c, axis=1, keepdims=True) + jnp.sum(p_n, axis=1, keepdims=True)
    o_st = (_mm(p_c.astype(BF16), vc.astype(BF16)) + _mm(p_n.astype(BF16), vn.astype(BF16))) / den
    o_st = jnp.where(own, o_st, 0.0)
    out = o_st[0:ts]
    for h in range(1, N_HEADS):
        out = out + o_st[h * ts:(h + 1) * ts]
    o_ref[...] = out
    ko_ref[0:w_buf - ts, :] = kc[ts:w_buf]
    ko_ref[w_buf - ts:w_buf, :] = kn
    vo_ref[0:w_buf - ts, :] = vc[ts:w_buf]
    vo_ref[w_buf - ts:w_buf, :] = vn


def _branch_counts(ts, w_buf):
    delta = (w_buf + jnp.arange(ts)[:, None]) - jnp.arange(w_buf + ts)[None, :]
    cnt = jnp.zeros(delta.shape, F32)
    for d in DILATIONS:
        cnt = cnt + ((delta >= 0) & (delta % d == 0) & (delta // d <= N_STEPS)).astype(F32)
    cnt = jnp.tile(cnt, (N_HEADS, 1))
    return cnt[:, :w_buf], cnt[:, w_buf:]


def _attn_sample(q, kn, vn, kcache, vcache):
    b, ts, _ = q.shape
    w_buf = kcache.shape[1]
    assert w_buf >= N_STEPS * max(DILATIONS) and ts % SUBLANES == 0
    cnt_c, cnt_n = _branch_counts(ts, w_buf)
    new = pl.BlockSpec((None, ts, ATT_W), lambda i: (i, 0, 0))
    cache = pl.BlockSpec((None, w_buf, ATT_W), lambda i: (i, 0, 0))
    return pl.pallas_call(
        functools.partial(_attn_sample_kernel, ts=ts, w_buf=w_buf),
        grid=(b,),
        in_specs=[new, new, new, cache, cache,
                  _const_spec((N_HEADS * ts, w_buf)), _const_spec((N_HEADS * ts, ts))],
        out_specs=[new, cache, cache],
        out_shape=[jax.ShapeDtypeStruct((b, ts, ATT_W), F32),
                   jax.ShapeDtypeStruct((b, w_buf, ATT_W), F32),
                   jax.ShapeDtypeStruct((b, w_buf, ATT_W), F32)],
        compiler_params=_params(("parallel",)),
        name="attn_sample",
    )(q, kn, vn, kcache, vcache, cnt_c, cnt_n)


FFN_TF = 256
N_FCHUNK = D_FF // FFN_TF


def _mlp_kernel(*refs, tm, tiles_per_seq, seq_len, has_state):
    if has_state:
        (x_ref, rw_ref, att_ref, wout_ref, gmix_ref, gpre_ref, gpost_ref, wup_ref, cw_ref, cb_ref,
         wd_ref, p2_ref, y_ref, u_ref, act_scr) = refs
    else:
        (x_ref, rw_ref, att_ref, wout_ref, gmix_ref, gpre_ref, gpost_ref, wup_ref, cw_ref, cb_ref,
         wd_ref, y_ref, u_ref, act_scr, ubuf_scr, carry_scr) = refs

        @pl.when((pl.program_id(0) % tiles_per_seq) == 0)
        def _():
            carry_scr[...] = jnp.zeros_like(carry_scr)

    mix = (_mm(rw_ref[...].astype(BF16), wout_ref[:RWKV_W, :])
           + _mm(att_ref[...].astype(BF16), wout_ref[RWKV_W:, :]))
    x1 = x_ref[...] + _rms(mix, gmix_ref[...])
    hb = _rms(x1, gpre_ref[...]).astype(BF16)
    t_in = lax.broadcasted_iota(jnp.int32, (tm, 1), 0) % seq_len

    for c in range(N_FCHUNK):
        conv = []
        for idx, col0 in enumerate((c * FFN_TF, D_FF + c * FFN_TF)):
            cols = slice(col0, col0 + FFN_TF)
            u = _mm(hb, wup_ref[:, cols])
            if has_state:
                p2 = p2_ref[:, cols]
                u1 = jnp.where(t_in >= 1, pltpu.roll(u, 1, 0), pltpu.roll(p2, tm - 1, 0))
                u2 = jnp.where(t_in >= 2, pltpu.roll(u, 2, 0), p2)
                u_ref[:, cols] = u
            else:
                stage = ubuf_scr.at[c % 2, idx]
                stage[0:SUBLANES, :] = carry_scr[c, idx]
                stage[SUBLANES:, :] = u
                u1 = stage[SUBLANES - 1:SUBLANES - 1 + tm, :]
                u2 = stage[SUBLANES - 2:SUBLANES - 2 + tm, :]
                carry_scr[c, idx] = u[tm - SUBLANES:]
                u_ref[:, cols] = u[tm - SUBLANES:]
            conv.append(u2 * cw_ref[0:1, cols] + u1 * cw_ref[1:2, cols] + u * cw_ref[2:3, cols]
                        + cb_ref[:, cols])
        act_scr[:, c * FFN_TF:(c + 1) * FFN_TF] = (conv[0] * _sigmoid(conv[0]) * conv[1]).astype(BF16)

    y_ref[...] = x1 + _rms(_mm(act_scr[...], wd_ref[...]), gpost_ref[...])


def _mlp(x2d, rw, att, w, tm, seq_len, state_rows=None):
    n = x2d.shape[0]
    has_state = state_rows is not None
    tiles_per_seq = max(seq_len // tm, 1)
    row = lambda i: (i, 0)
    vec = lambda: _const_spec((1, D_MODEL))
    in_specs = [pl.BlockSpec((tm, D_MODEL), row), pl.BlockSpec((tm, RWKV_W), row),
                pl.BlockSpec((tm, ATT_W), row), _const_spec((D_MODEL, D_MODEL)), vec(), vec(), vec(),
                _const_spec((D_MODEL, 2 * D_FF)), _const_spec((CONV_W, 2 * D_FF)),
                _const_spec((1, 2 * D_FF)), _const_spec((D_FF, D_MODEL))]
    args = [x2d, rw, att, w["w_out"], w["norm_mix_post"], w["norm_ffn_pre"], w["norm_ffn_post"],
            w["w_ffn_up"], w["ffn_conv_w"], w["ffn_conv_b"], w["w_ffn_down"]]
    scratch = [pltpu.VMEM((tm, D_FF), BF16)]
    if has_state:
        assert tm % seq_len == 0 and seq_len == SUBLANES
        in_specs.append(pl.BlockSpec((tm, 2 * D_FF), row))
        args.append(state_rows)
        u_spec = pl.BlockSpec((tm, 2 * D_FF), row)
        u_shape = jax.ShapeDtypeStruct((n, 2 * D_FF), F32)
    else:
        assert seq_len % tm == 0
        scratch += [pltpu.VMEM((2, 2, tm + SUBLANES, FFN_TF), F32),
                    pltpu.VMEM((N_FCHUNK, 2, SUBLANES, FFN_TF), F32)]
        u_spec = pl.BlockSpec((None, SUBLANES, 2 * D_FF), lambda i: (i, 0, 0))
        u_shape = jax.ShapeDtypeStruct((n // tm, SUBLANES, 2 * D_FF), F32)
    kern = functools.partial(_mlp_kernel, tm=tm, tiles_per_seq=tiles_per_seq, seq_len=seq_len,
                             has_state=has_state)
    return pl.pallas_call(
        kern,
        grid=(n // tm,),
        in_specs=in_specs,
        out_specs=[pl.BlockSpec((tm, D_MODEL), row), u_spec],
        out_shape=[jax.ShapeDtypeStruct((n, D_MODEL), F32), u_shape],
        scratch_shapes=scratch,
        compiler_params=_params(("arbitrary",)),
        name="mlp",
    )(*args)


def _rope_tables(pos):
    half = HEAD_DIM // 2
    inv = ROPE_THETA ** (-jnp.arange(half, dtype=F32) / half)
    ang = pos.astype(F32)[:, None] * inv[None, :]
    cos, sin = jnp.cos(ang), jnp.sin(ang)
    cos_t = jnp.tile(jnp.concatenate([cos, cos], axis=1), (1, LANES // HEAD_DIM))
    sin_t = jnp.tile(jnp.concatenate([-sin, sin], axis=1), (1, LANES // HEAD_DIM))
    return cos_t, sin_t


def _layer(x, pos, h_prev, wkv0, k_buf, v_buf, conv_prev, w, tm, chunk, mlp_tm):
    b, t, _ = x.shape
    n = b * t
    x2d = x.reshape(n, D_MODEL)
    cos_t, sin_t = _rope_tables(pos)
    if t < tm:
        cos_t = jnp.tile(cos_t, (tm // t, 1))
        sin_t = jnp.tile(sin_t, (tm // t, 1))
        pos_tiles = 1
    else:
        pos_tiles = t // tm
    rw, q, k, v = _in_proj(x2d, w["norm_mix_pre"], w["w_in"], cos_t, sin_t, tm, pos_tiles)
    h_last, pstart = _shift_state(x[:, -1], w["norm_mix_pre"], h_prev, w["w_in_rw"])

    rwkv_out, wkv_new = _rwkv(rw.reshape(b, t, RWKV_COLS), pstart.reshape(b, 1, RWKV_COLS), wkv0,
                              w["rwkv"], chunk)
    q3, k3, v3 = (a.reshape(b, t, ATT_W) for a in (q, k, v))
    if k_buf is None:
        att = _attn_prompt(q3, k3, v3)
        keep = min(N_STEPS * max(DILATIONS), t)
        k_new, v_new = k3[:, -keep:], v3[:, -keep:]
    else:
        w_buf = k_buf.shape[1]
        att, k_new, v_new = _attn_sample(q3, k3, v3, k_buf.reshape(b, w_buf, ATT_W),
                                         v_buf.reshape(b, w_buf, ATT_W))
    rw2d, att2d = rwkv_out.reshape(n, RWKV_W), att.reshape(n, ATT_W)
    if conv_prev is None:
        y, tails = _mlp(x2d, rw2d, att2d, w, mlp_tm, t)
        conv_new = tails.reshape(b, t // mlp_tm, SUBLANES, 2 * D_FF)[:, -1, SUBLANES - (CONV_W - 1):]
    else:
        p2 = jnp.pad(conv_prev, ((0, 0), (0, t - (CONV_W - 1)), (0, 0))).reshape(n, 2 * D_FF)
        y, u = _mlp(x2d, rw2d, att2d, w, mlp_tm, t, state_rows=p2)
        conv_new = u.reshape(b, t, 2 * D_FF)[:, t - (CONV_W - 1):]
    heads = lambda a: a.reshape(b, a.shape[1], N_HEADS, HEAD_DIM)
    return y.reshape(b, t, D_MODEL), h_last, wkv_new, heads(k_new), heads(v_new), conv_new


def _prep_weights(norm_mix_pre, norm_mix_post, norm_ffn_pre, norm_ffn_post, w_in, mu_shift, w0,
                  w_decay_up, a0, w_iclr_up, w_gate_up, k_k, k_a, r_k, lnx_w, lnx_b, w_out,
                  w_ffn_up, ffn_conv_w, ffn_conv_b, w_ffn_down):
    vec = lambda a: a.reshape(1, -1)
    zero = jnp.zeros((D_DECAY, RWKV_W), F32)
    lora = jnp.concatenate([jnp.concatenate([w_decay_up, zero], axis=1),
                            jnp.concatenate([zero, w_iclr_up], axis=1)], axis=0)
    lora_hi, lora_lo = _split(lora)
    wg_hi, wg_lo = _split(w_gate_up)
    w_in_bf = w_in.astype(BF16)
    return {
        "norm_mix_pre": vec(norm_mix_pre), "norm_mix_post": vec(norm_mix_post),
        "norm_ffn_pre": vec(norm_ffn_pre), "norm_ffn_post": vec(norm_ffn_post),
        "w_in": w_in_bf, "w_in_rw": w_in_bf[:, :RWKV_COLS],
        "rwkv": (vec(mu_shift), vec(w0), vec(a0), lora_hi, lora_lo, wg_hi, wg_lo,
                 vec(k_k), vec(k_a), vec(r_k), vec(lnx_w), vec(lnx_b)),
        "w_out": w_out.astype(BF16), "w_ffn_up": w_ffn_up.astype(BF16),
        "ffn_conv_w": ffn_conv_w, "ffn_conv_b": vec(ffn_conv_b), "w_ffn_down": w_ffn_down.astype(BF16),
    }


PROMPT_TM = 512
PROMPT_CHUNK = 64
SAMPLE_MLP_TM = 128


def kernel(x_prompt, x_sample, state_rwkv_shift, state_rwkv_wkv, cache_att_k, cache_att_v, state_ffn_conv, norm_mix_pre, norm_mix_post, norm_ffn_pre, norm_ffn_post, w_in, mu_shift, w0, w_decay_up, a0, w_iclr_up, w_gate_up, k_k, k_a, r_k, lnx_w, lnx_b, w_out, w_ffn_up, ffn_conv_w, ffn_conv_b, w_ffn_down):
    bp, tp, _ = x_prompt.shape
    bs, ts, _ = x_sample.shape
    depth = norm_mix_pre.shape[0]
    pos_p = jnp.arange(tp, dtype=jnp.int32)
    pos_s = PAST_LEN + jnp.arange(ts, dtype=jnp.int32)
    yp, ys = x_prompt, x_sample
    outs_p = [[] for _ in range(5)]
    outs_s = [[] for _ in range(5)]
    for l in range(depth):
        w = _prep_weights(norm_mix_pre[l], norm_mix_post[l], norm_ffn_pre[l], norm_ffn_post[l], w_in[l],
                          mu_shift[l], w0[l], w_decay_up[l], a0[l], w_iclr_up[l], w_gate_up[l], k_k[l],
                          k_a[l], r_k[l], lnx_w[l], lnx_b[l], w_out[l], w_ffn_up[l], ffn_conv_w[l],
                          ffn_conv_b[l], w_ffn_down[l])
        yp, *state_p = _layer(yp, pos_p, jnp.zeros((bp, D_MODEL), F32),
                              jnp.zeros((bp, N_HEADS, HEAD_DIM, HEAD_DIM), F32), None, None, None,
                              w, min(PROMPT_TM, tp), min(PROMPT_CHUNK, tp), min(PROMPT_TM, tp))
        ys, *state_s = _layer(ys, pos_s, state_rwkv_shift[l], state_rwkv_wkv[l], cache_att_k[l],
                              cache_att_v[l], state_ffn_conv[l], w, bs * ts, ts, SAMPLE_MLP_TM)
        for acc, val in zip(outs_p, state_p):
            acc.append(val)
        for acc, val in zip(outs_s, state_s):
            acc.append(val)
    return (yp, ys, *(jnp.stack(a) for a in outs_p), *(jnp.stack(a) for a in outs_s))
```

```python
import functools

import jax
import jax.numpy as jnp
from jax import lax
from jax.experimental import pallas as pl
from jax.experimental.pallas import tpu as pltpu

F32 = jnp.float32
BF16 = jnp.bfloat16

D_MODEL = 1024
HEAD_DIM = 64
RWKV_W = 512
ATT_W = 512
N_HEADS = 8
N_PAIRS = N_HEADS // 2
D_DECAY = 64
D_ICLR = 64
D_GATE = 128
RWKV_COLS = 3 * RWKV_W + D_DECAY + D_ICLR + D_GATE
D_IN = RWKV_COLS + 3 * ATT_W
DILATIONS = (1, 4, 16)
N_STEPS = 128
ROPE_THETA = 10000.0
D_FF = 2816
CONV_W = 3
NORM_EPS = 1e-6
LNX_EPS = 64e-5
NEG_INF = -1e30
PAST_LEN = 16384

LANES = 128
SUBLANES = 8
VMEM_LIMIT = 56 * 1024 * 1024

NN = (((1,), (0,)), ((), ()))
NT = (((1,), (1,)), ((), ()))
TN = (((0,), (0,)), ((), ()))


def _mm(a, b, dims=NN):
    return lax.dot_general(a, b, dims, preferred_element_type=F32)


def _split(x):
    hi = x.astype(BF16)
    lo = (x - hi.astype(F32)).astype(BF16)
    return hi, lo


def _mm1(a, b, dims=NN):
    return _mm(a.astype(BF16), b.astype(BF16), dims)


def _mm3(a, b, dims=NN):
    ah, al = _split(a)
    bh, bl = _split(b)
    return _mm(ah, bh, dims) + (_mm(ah, bl, dims) + _mm(al, bh, dims))


_mm_gram = _mm1
_mm_inv = _mm1
_mm_apply = _mm1
_mm_state = _mm1


def _mm_split_lhs(a, b_bf16):
    ah, al = _split(a)
    return _mm(ah, b_bf16) + _mm(al, b_bf16)


def _rms(x, g):
    return x * lax.rsqrt(jnp.mean(x * x, axis=-1, keepdims=True) + NORM_EPS) * g


def _sigmoid(x):
    return 1.0 / (1.0 + jnp.exp(-x))


def _params(sem):
    return pltpu.CompilerParams(dimension_semantics=sem, vmem_limit_bytes=VMEM_LIMIT)


def _const_spec(shape):
    nd = len(shape)
    return pl.BlockSpec(shape, lambda *_: (0,) * nd, pipeline_mode=pl.Buffered(1))


def _in_proj_kernel(x_ref, g_ref, w_ref, cos_ref, sin_ref, rw_ref, q_ref, k_ref, v_ref, *tail_refs):
    lane = lax.broadcasted_iota(jnp.int32, (1, ATT_W), 1)
    first_half = (lane % HEAD_DIM) < HEAD_DIM // 2
    q0 = RWKV_COLS
    hb = _rms(x_ref[...], g_ref[...]).astype(BF16)
    rw_ref[...] = _mm(hb, w_ref[:, :RWKV_COLS]).astype(BF16)
    cos = jnp.concatenate([cos_ref[...]] * (ATT_W // LANES), axis=1)
    sin = jnp.concatenate([sin_ref[...]] * (ATT_W // LANES), axis=1)

    def rope(t):
        partner = jnp.where(first_half, pltpu.roll(t, ATT_W - HEAD_DIM // 2, 1),
                            pltpu.roll(t, HEAD_DIM // 2, 1))
        return t * cos + partner * sin

    q_ref[...] = rope(_mm(hb, w_ref[:, q0:q0 + ATT_W]))
    k = rope(_mm(hb, w_ref[:, q0 + ATT_W:q0 + 2 * ATT_W]))
    v = _mm(hb, w_ref[:, q0 + 2 * ATT_W:q0 + 3 * ATT_W])
    k_ref[...] = k
    v_ref[...] = v
    if tail_refs:
        tail_refs[0][...] = k
        tail_refs[1][...] = v


def _in_proj(x2d, g, w_bf, cos_t, sin_t, tm, pos_tiles, tail_tiles=0):
    n = x2d.shape[0]
    row = lambda i: (i, 0)
    pos = lambda i: (i % pos_tiles, 0)
    out_specs = [pl.BlockSpec((tm, RWKV_COLS), row)] + [pl.BlockSpec((tm, ATT_W), row)] * 3
    out_shape = [jax.ShapeDtypeStruct((n, RWKV_COLS), BF16)] + [jax.ShapeDtypeStruct((n, ATT_W), F32)] * 3
    if tail_tiles:
        skip = pos_tiles - tail_tiles
        tail = lambda i: ((i // pos_tiles) * tail_tiles + jnp.maximum(i % pos_tiles - skip, 0), 0)
        out_specs += [pl.BlockSpec((tm, ATT_W), tail)] * 2
        out_shape += [jax.ShapeDtypeStruct((n // pos_tiles * tail_tiles, ATT_W), F32)] * 2
    return pl.pallas_call(
        _in_proj_kernel,
        grid=(n // tm,),
        in_specs=[pl.BlockSpec((tm, D_MODEL), row), _const_spec((1, D_MODEL)),
                  _const_spec((D_MODEL, D_IN)),
                  pl.BlockSpec((tm, LANES), pos), pl.BlockSpec((tm, LANES), pos)],
        out_specs=out_specs,
        out_shape=out_shape,
        compiler_params=_params(("arbitrary",)),
        name="in_proj",
    )(x2d, g, w_bf, cos_t, sin_t)


def _shift_state_kernel(x_ref, g_ref, h_ref, w_ref, hl_ref, ps_ref):
    hl_ref[...] = _rms(x_ref[...], g_ref[...])
    ps_ref[...] = _mm(h_ref[...].astype(BF16), w_ref[...])


def _shift_state(x_last, g, h_prev, w_rw_bf):
    b = x_last.shape[0]
    return pl.pallas_call(
        _shift_state_kernel,
        out_shape=[jax.ShapeDtypeStruct((b, D_MODEL), F32), jax.ShapeDtypeStruct((b, RWKV_COLS), F32)],
        compiler_params=_params(None),
        name="shift_state",
    )(x_last, g, h_prev, w_rw_bf)


def _rwkv_kernel(p_ref, ps_ref, s0_ref, mu_ref, w0_ref, a0_ref, lora_hi_ref, lora_lo_ref,
                 wg_hi_ref, wg_lo_ref, kk_ref, ka_ref, rk_ref, lnw_ref, lnb_ref,
                 y_ref, sfin_ref, s_scr, prev_scr, *, C, n_chunks, nb):
    ci = pl.program_id(1)
    C2 = 2 * C
    rows = nb * C
    seqs = range(nb)
    rsl = [slice(C * s, C * (s + 1)) for s in seqs]

    @pl.when(ci == 0)
    def _():
        prev_scr[...] = ps_ref[...]
        z = jnp.zeros((HEAD_DIM, HEAD_DIM), F32)
        for s in seqs:
            for p in range(N_PAIRS):
                top = jnp.concatenate([s0_ref[s, 2 * p], z], axis=1)
                bot = jnp.concatenate([z, s0_ref[s, 2 * p + 1]], axis=1)
                s_scr[s * N_PAIRS + p] = jnp.concatenate([top, bot], axis=0)

    def per_seq(vals):
        return jnp.concatenate([jnp.broadcast_to(x, (C, x.shape[-1])) for x in vals], axis=0)

    P = p_ref[...].astype(F32).reshape(rows, RWKV_COLS)
    row_c = lax.broadcasted_iota(jnp.int32, (rows, 1), 0) % C
    shifted = jnp.where(row_c == 0, per_seq([prev_scr[s] for s in seqs]), pltpu.roll(P, 1, 0))
    for s in seqs:
        prev_scr[s] = P[C * (s + 1) - 1:C * (s + 1), :]
    mixed = P + (shifted - P) * mu_ref[...]

    r = mixed[:, 0:RWKV_W]
    k = mixed[:, RWKV_W:2 * RWKV_W]
    v = mixed[:, 2 * RWKV_W:3 * RWKV_W]
    xl = mixed[:, 3 * RWKV_W:3 * RWKV_W + LANES]
    gl = mixed[:, 3 * RWKV_W + LANES:RWKV_COLS]

    lane = lax.broadcasted_iota(jnp.int32, (1, LANES), 1)
    lane_lo = lane < HEAD_DIM
    th, tl = _split(jnp.where(lane_lo, jnp.tanh(xl), xl))
    lora = _mm(th, lora_hi_ref[...]) + (_mm(th, lora_lo_ref[...]) + _mm(tl, lora_hi_ref[...]))
    w_pre = w0_ref[...] + lora[:, :RWKV_W]
    a_pre = a0_ref[...] + lora[:, RWKV_W:]
    z = -w_pre
    softplus = jnp.maximum(z, 0.0) + jnp.log1p(jnp.exp(-jnp.abs(z)))
    logd = -jnp.exp(-softplus - 0.5)
    a_lr = _sigmoid(a_pre)
    sgh, sgl = _split(_sigmoid(gl))
    gate = _mm(sgh, wg_hi_ref[...]) + (_mm(sgh, wg_lo_ref[...]) + _mm(sgl, wg_hi_ref[...]))

    ri = lax.broadcasted_iota(jnp.int32, (LANES, LANES), 0)
    cj = lax.broadcasted_iota(jnp.int32, (LANES, LANES), 1)
    seg_ones = jnp.where((ri < HEAD_DIM) == (cj < HEAD_DIM), 1.0, 0.0).astype(BF16)

    def seg_sum(x):
        return jnp.concatenate(
            [_mm_split_lhs(x[:, LANES * p:LANES * (p + 1)], seg_ones) for p in range(N_PAIRS)], axis=1)

    kkr = k * kk_ref[...]
    kkn = kkr / jnp.maximum(jnp.sqrt(seg_sum(kkr * kkr)), 1e-12)
    kmod = k * (1.0 + (a_lr - 1.0) * ka_ref[...])
    bvec = kkn * a_lr
    avec = -kkn

    c = logd
    s = 1
    while s < C:
        c = c + jnp.where(row_c >= s, pltpu.roll(c, s, 0), 0.0)
        s *= 2
    mid = C // 2 - 1
    c_mid = [c[C * s + mid:C * s + mid + 1, :] for s in seqs]
    c_end = [c[C * (s + 1) - 1:C * (s + 1), :] for s in seqs]
    cp = c - per_seq(c_mid)
    e_pos = jnp.exp(cp)
    e_neg = jnp.exp(-cp)
    e_prev = jnp.exp(cp - logd)
    e_mid = [jnp.exp(x) for x in c_mid]
    e_end = [jnp.exp(x) for x in c_end]
    e_endp = [jnp.exp(c_end[s] - c_mid[s]) for s in seqs]
    a_t = avec * e_prev
    r_t = r * e_pos
    b_t = bvec * e_neg
    k_t = kmod * e_neg
    bonus_in = r * kmod * rk_ref[...]

    def stack(x):
        return jnp.concatenate([jnp.where(lane_lo, x, 0.0), jnp.where(lane_lo, 0.0, x)], axis=0)

    ri2 = lax.broadcasted_iota(jnp.int32, (C2, C2), 0)
    cj2 = lax.broadcasted_iota(jnp.int32, (C2, C2), 1)
    strict = cj2 < ri2
    incl = cj2 <= ri2
    n_sq = C.bit_length() - 2
    fused = C2 % LANES == 0
    unit_seq = [s for s in seqs for _ in range(N_PAIRS)]
    pairs = range(nb * N_PAIRS)
    sls = [slice(LANES * p, LANES * (p + 1)) for _ in seqs for p in range(N_PAIRS)]
    rws = [rsl[s] for s in unit_seq]
    e_mid = [e_mid[s] for s in unit_seq]
    e_end = [e_end[s] for s in unit_seq]
    e_endp = [e_endp[s] for s in unit_seq]

    ar = [jnp.concatenate([stack(a_t[rws[p], sls[p]]), stack(r_t[rws[p], sls[p]])], axis=0)
          for p in pairs]
    bk = [jnp.concatenate([stack(b_t[rws[p], sls[p]]), stack(k_t[rws[p], sls[p]])], axis=0) for p in pairs]
    v_st = [stack(v[rws[p], sls[p]]) for p in pairs]
    if fused:
        gram = [_mm_gram(ar[p], bk[p], NT) for p in pairs]
        g_ab = [jnp.where(strict, g[:C2, :C2], 0.0) for g in gram]
        g_ak = [jnp.where(strict, g[:C2, C2:], 0.0) for g in gram]
        g_rb = [jnp.where(incl, g[C2:, :C2], 0.0) for g in gram]
        g_rk = [jnp.where(incl, g[C2:, C2:], 0.0) for g in gram]
    else:
        g_ab = [jnp.where(strict, _mm_gram(ar[p][:C2], bk[p][:C2], NT), 0.0) for p in pairs]
        g_ak = [jnp.where(strict, _mm_gram(ar[p][:C2], bk[p][C2:], NT), 0.0) for p in pairs]
        g_rb = [jnp.where(incl, _mm_gram(ar[p][C2:], bk[p][:C2], NT), 0.0) for p in pairs]
        g_rk = [jnp.where(incl, _mm_gram(ar[p][C2:], bk[p][C2:], NT), 0.0) for p in pairs]

    z2 = [_mm_apply(g_ak[p], v_st[p]) for p in pairs]
    x = [jnp.concatenate([ar[p][:C2] * e_mid[p][:, sls[p]], z2[p]], axis=1) for p in pairs]
    lp = g_ab
    x = [x[p] + _mm_apply(lp[p], x[p]) for p in pairs]
    for _ in range(n_sq):
        lp = [_mm_inv(lp[p], lp[p]) for p in pairs]
        x = [x[p] + _mm_apply(lp[p], x[p]) for p in pairs]

    s_old = [s_scr[p] for p in pairs]
    wr = [jnp.concatenate([x[p][:, :LANES], ar[p][C2:] * e_mid[p][:, sls[p]]], axis=0) for p in pairs]
    ws = [_mm_state(wr[p], s_old[p], NT) for p in pairs]
    u = [ws[p][:C2] + x[p][:, LANES:] for p in pairs]
    uv = [jnp.concatenate([u[p], v_st[p]], axis=0) for p in pairs]
    if fused:
        y_st = [ws[p][C2:] + _mm_state(jnp.concatenate([g_rb[p], g_rk[p]], axis=1), uv[p]) for p in pairs]
    else:
        y_st = [ws[p][C2:] + _mm_state(g_rb[p], u[p]) + _mm_state(g_rk[p], v_st[p]) for p in pairs]
    for p in pairs:
        s_scr[p] = s_old[p] * e_end[p][:, sls[p]] + _mm_state(uv[p], bk[p] * e_endp[p][:, sls[p]], TN)

    y = [jnp.concatenate([y_st[s * N_PAIRS + p][:C] + y_st[s * N_PAIRS + p][C:] for s in seqs], axis=0)
         for p in range(N_PAIRS)]
    for p in range(N_PAIRS):
        sl = sls[p]
        dev = y[p] - _mm_split_lhs(y[p], seg_ones) * (1.0 / HEAD_DIM)
        var = _mm_split_lhs(dev * dev, seg_ones) * (1.0 / HEAD_DIM)
        yn = dev * lax.rsqrt(var + LNX_EPS) * lnw_ref[:, sl] + lnb_ref[:, sl]
        bonus = _mm_split_lhs(bonus_in[:, sl], seg_ones) * v[:, sl]
        y_ref[:, :, sl] = ((yn + bonus) * gate[:, sl]).reshape(nb, C, LANES)

    @pl.when(ci == n_chunks - 1)
    def _():
        for s in seqs:
            for p in range(N_PAIRS):
                s_fin = s_scr[s * N_PAIRS + p]
                sfin_ref[s, 2 * p] = s_fin[:HEAD_DIM, :HEAD_DIM]
                sfin_ref[s, 2 * p + 1] = s_fin[HEAD_DIM:, HEAD_DIM:]


RWKV_SEQS = 8


def _rwkv(proj_rw, pstart, s0, wts, C):
    b, t, _ = proj_rw.shape
    nb = RWKV_SEQS
    assert b % nb == 0 and t % C == 0
    n_chunks = t // C
    vec = lambda n: _const_spec((1, n))
    kern = functools.partial(_rwkv_kernel, C=C, n_chunks=n_chunks, nb=nb)
    return pl.pallas_call(
        kern,
        grid=(b // nb, n_chunks),
        in_specs=[pl.BlockSpec((nb, C, RWKV_COLS), lambda i, j: (i, j, 0)),
                  pl.BlockSpec((nb, 1, RWKV_COLS), lambda i, j: (i, 0, 0)),
                  pl.BlockSpec((nb, N_HEADS, HEAD_DIM, HEAD_DIM), lambda i, j: (i, 0, 0, 0)),
                  vec(RWKV_COLS), vec(RWKV_W), vec(RWKV_W),
                  _const_spec((LANES, 2 * RWKV_W)), _const_spec((LANES, 2 * RWKV_W)),
                  _const_spec((D_GATE, RWKV_W)), _const_spec((D_GATE, RWKV_W)),
                  vec(RWKV_W), vec(RWKV_W), vec(RWKV_W), vec(RWKV_W), vec(RWKV_W)],
        out_specs=[pl.BlockSpec((nb, C, RWKV_W), lambda i, j: (i, j, 0)),
                   pl.BlockSpec((nb, N_HEADS, HEAD_DIM, HEAD_DIM), lambda i, j: (i, 0, 0, 0))],
        out_shape=[jax.ShapeDtypeStruct((b, t, RWKV_W), F32),
                   jax.ShapeDtypeStruct((b, N_HEADS, HEAD_DIM, HEAD_DIM), F32)],
        scratch_shapes=[pltpu.VMEM((nb * N_PAIRS, LANES, LANES), F32), pltpu.VMEM((nb, 1, RWKV_COLS), F32)],
        compiler_params=_params(("parallel", "arbitrary")),
        name="rwkv",
    )(proj_rw, pstart, s0, *wts)


ATT_UNROLL = 4


def _attn_prompt_kernel(q_ref, k_ref, v_ref, bias_ref, o_ref, acc_scr, m_scr, l_scr, *, T):
    QB = N_STEPS
    lane = lax.broadcasted_iota(jnp.int32, (1, LANES), 1)
    lane_lo = lane < HEAD_DIM

    def halves(x):
        return jnp.where(lane_lo, x[:QB], x[QB:])

    order = sorted(DILATIONS, reverse=True)
    for bi, d in enumerate(order):
        per_res = T // (QB * d)
        ub = min(ATT_UNROLL, per_res)
        groups = ATT_UNROLL // ub
        gpr = per_res // ub

        def rows(start, d=d):
            if d == 1:
                return pl.ds(pl.multiple_of(start, QB), QB)
            return pl.ds(start, QB, stride=d)

        def body(it, carry, bi=bi, d=d, ub=ub, groups=groups, gpr=gpr, rows=rows):
            blocks = []
            for g in range(groups):
                gi = it * groups + g
                res = gi // gpr
                n0 = (gi % gpr) * ub
                starts = [res + d * QB * jnp.maximum(n0 - 1, 0)] + [res + d * QB * (n0 + u) for u in range(ub)]
                kb = [k_ref[rows(s), :].astype(BF16) for s in starts]
                vb = [v_ref[rows(s), :].astype(BF16) for s in starts]
                for u in range(ub):
                    bias = bias_ref[jnp.where(n0 == 0, 1, 0)] if u == 0 else bias_ref[0]
                    blocks.append((starts[u + 1], jnp.concatenate([kb[u], kb[u + 1]], axis=0),
                                   jnp.concatenate([vb[u], vb[u + 1]], axis=0), bias))
            q_st = []
            for start, _, _, _ in blocks:
                q = q_ref[rows(start), :] * (HEAD_DIM ** -0.5)
                q_st.append(jnp.concatenate([jnp.where(lane_lo, q, 0.0), jnp.where(lane_lo, 0.0, q)],
                                            axis=0).astype(BF16))
            s = [_mm(q_st[i], blk[1], NT) + blk[3] for i, blk in enumerate(blocks)]
            m = [jnp.max(x, axis=1, keepdims=True) for x in s]
            p = [jnp.exp(s[i] - m[i]) for i in range(len(blocks))]
            l = [jnp.sum(x, axis=1, keepdims=True) for x in p]
            o = [halves(_mm(p[i].astype(BF16), blk[2])) for i, blk in enumerate(blocks)]
            m_b = [halves(jnp.broadcast_to(x, (2 * QB, LANES))) for x in m]
            l_b = [halves(jnp.broadcast_to(x, (2 * QB, LANES))) for x in l]
            if bi == 0:
                for i, blk in enumerate(blocks):
                    acc_scr[rows(blk[0]), :] = o[i]
                    m_scr[rows(blk[0]), :] = m_b[i]
                    l_scr[rows(blk[0]), :] = l_b[i]
            else:
                m_old = [m_scr[rows(blk[0]), :] for blk in blocks]
                acc_old = [acc_scr[rows(blk[0]), :] for blk in blocks]
                l_old = [l_scr[rows(blk[0]), :] for blk in blocks]
                for i, blk in enumerate(blocks):
                    m_new = jnp.maximum(m_old[i], m_b[i])
                    w_old = jnp.exp(m_old[i] - m_new)
                    w_new = jnp.exp(m_b[i] - m_new)
                    acc = acc_old[i] * w_old + o[i] * w_new
                    den = l_old[i] * w_old + l_b[i] * w_new
                    if bi < len(order) - 1:
                        acc_scr[rows(blk[0]), :] = acc
                        m_scr[rows(blk[0]), :] = m_new
                        l_scr[rows(blk[0]), :] = den
                    else:
                        o_ref[rows(blk[0]), :] = acc / den
            return carry

        lax.fori_loop(0, d * gpr // groups, body, 0)


def _band_bias():
    qi = jnp.arange(2 * N_STEPS)[:, None] % N_STEPS
    kj = jnp.arange(2 * N_STEPS)[None, :]
    band = (kj >= qi) & (kj <= qi + N_STEPS)
    normal = jnp.where(band, 0.0, NEG_INF).astype(F32)
    first = jnp.where(band & (kj >= N_STEPS), 0.0, NEG_INF).astype(F32)
    return jnp.stack([normal, first])


def _attn_prompt(q, k, v):
    b, t, _ = q.shape
    assert t % (N_STEPS * max(DILATIONS)) == 0
    spec = pl.BlockSpec((None, t, LANES), lambda i, p: (i, 0, p))
    return pl.pallas_call(
        functools.partial(_attn_prompt_kernel, T=t),
        grid=(b, N_PAIRS),
        in_specs=[spec, spec, spec, _const_spec((2, 2 * N_STEPS, 2 * N_STEPS))],
        out_specs=spec,
        out_shape=jax.ShapeDtypeStruct((b, t, ATT_W), F32),
        scratch_shapes=[pltpu.VMEM((t, LANES), F32)] * 3,
        compiler_params=_params(("parallel", "parallel")),
        name="attn_prompt",
    )(q, k, v, _band_bias())


def _attn_sample_kernel(q_ref, kn_ref, vn_ref, kc_ref, vc_ref, cntc_ref, cntn_ref,
                        o_ref, ko_ref, vo_ref, *, ts, w_buf):
    rows = N_HEADS * ts
    ri = lax.broadcasted_iota(jnp.int32, (rows, ATT_W), 0)
    cj = lax.broadcasted_iota(jnp.int32, (rows, ATT_W), 1)
    own = (ri // ts) == (cj // HEAD_DIM)
    q_all = jnp.concatenate([q_ref[...] * (HEAD_DIM ** -0.5)] * N_HEADS, axis=0)
    q_st = jnp.where(own, q_all, 0.0).astype(BF16)
    kc = kc_ref[...]
    vc = vc_ref[...]
    kn = kn_ref[...]
    vn = vn_ref[...]
    cnt_c = cntc_ref[...]
    cnt_n = cntn_ref[...]
    s_c = jnp.where(cnt_c > 0.0, _mm(q_st, kc.astype(BF16), NT), NEG_INF)
    s_n = jnp.where(cnt_n > 0.0, _mm(q_st, kn.astype(BF16), NT), NEG_INF)
    m = jnp.maximum(jnp.max(s_c, axis=1, keepdims=True), jnp.max(s_n, axis=1, keepdims=True))
    p_c = cnt_c * jnp.exp(s_c - m)
    p_n = cnt_n * jnp.exp(s_n - m)
    den = jnp.sum(p_c, axis=1, keepdims=True) + jnp.sum(p_n, axis=1, keepdims=True)
    o_st = (_mm(p_c.astype(BF16), vc.astype(BF16)) + _mm(p_n.astype(BF16), vn.astype(BF16))) / den
    o_st = jnp.where(own, o_st, 0.0)
    out = o_st[0:ts]
    for h in range(1, N_HEADS):
        out = out + o_st[h * ts:(h + 1) * ts]
    o_ref[...] = out
    ko_ref[0:w_buf - ts, :] = kc[ts:w_buf]
    ko_ref[w_buf - ts:w_buf, :] = kn
    vo_ref[0:w_buf - ts, :] = vc[ts:w_buf]
    vo_ref[w_buf - ts:w_buf, :] = vn


def _branch_counts(ts, w_buf):
    delta = (w_buf + jnp.arange(ts)[:, None]) - jnp.arange(w_buf + ts)[None, :]
    cnt = jnp.zeros(delta.shape, F32)
    for d in DILATIONS:
        cnt = cnt + ((delta >= 0) & (delta % d == 0) & (delta // d <= N_STEPS)).astype(F32)
    cnt = jnp.tile(cnt, (N_HEADS, 1))
    return cnt[:, :w_buf], cnt[:, w_buf:]


def _attn_sample(q, kn, vn, kcache, vcache):
    b, ts, _ = q.shape
    w_buf = kcache.shape[1]
    assert w_buf >= N_STEPS * max(DILATIONS) and ts % SUBLANES == 0
    cnt_c, cnt_n = _branch_counts(ts, w_buf)
    new = pl.BlockSpec((None, ts, ATT_W), lambda i: (i, 0, 0))
    cache = pl.BlockSpec((None, w_buf, ATT_W), lambda i: (i, 0, 0))
    return pl.pallas_call(
        functools.partial(_attn_sample_kernel, ts=ts, w_buf=w_buf),
        grid=(b,),
        in_specs=[new, new, new, cache, cache,
                  _const_spec((N_HEADS * ts, w_buf)), _const_spec((N_HEADS * ts, ts))],
        out_specs=[new, cache, cache],
        out_shape=[jax.ShapeDtypeStruct((b, ts, ATT_W), F32),
                   jax.ShapeDtypeStruct((b, w_buf, ATT_W), F32),
                   jax.ShapeDtypeStruct((b, w_buf, ATT_W), F32)],
        compiler_params=_params(("parallel",)),
        name="attn_sample",
    )(q, kn, vn, kcache, vcache, cnt_c, cnt_n)


FFN_TF = 256
N_FCHUNK = D_FF // FFN_TF


def _mlp_kernel(*refs, tm, tiles_per_seq, seq_len, has_state):
    if has_state:
        (x_ref, rw_ref, att_ref, wout_ref, gmix_ref, gpre_ref, gpost_ref, wup_ref, cw_ref, cb_ref,
         wd_ref, p2_ref, y_ref, u_ref, act_scr) = refs
    else:
        (x_ref, rw_ref, att_ref, wout_ref, gmix_ref, gpre_ref, gpost_ref, wup_ref, cw_ref, cb_ref,
         wd_ref, y_ref, u_ref, act_scr, ubuf_scr, carry_scr) = refs

        @pl.when((pl.program_id(0) % tiles_per_seq) == 0)
        def _():
            carry_scr[...] = jnp.zeros_like(carry_scr)

    mix = (_mm(rw_ref[...].astype(BF16), wout_ref[:RWKV_W, :])
           + _mm(att_ref[...].astype(BF16), wout_ref[RWKV_W:, :]))
    x1 = x_ref[...] + _rms(mix, gmix_ref[...])
    hb = _rms(x1, gpre_ref[...]).astype(BF16)
    t_in = lax.broadcasted_iota(jnp.int32, (tm, 1), 0) % seq_len

    for c in range(N_FCHUNK):
        conv = []
        for idx, col0 in enumerate((c * FFN_TF, D_FF + c * FFN_TF)):
            cols = slice(col0, col0 + FFN_TF)
            u = _mm(hb, wup_ref[:, cols])
            if has_state:
                p2 = p2_ref[:, cols]
                u1 = jnp.where(t_in >= 1, pltpu.roll(u, 1, 0), pltpu.roll(p2, tm - 1, 0))
                u2 = jnp.where(t_in >= 2, pltpu.roll(u, 2, 0), p2)
                u_ref[:, cols] = u
            else:
                stage = ubuf_scr.at[c % 2, idx]
                stage[0:SUBLANES, :] = carry_scr[c, idx]
                stage[SUBLANES:, :] = u
                u1 = stage[SUBLANES - 1:SUBLANES - 1 + tm, :]
                u2 = stage[SUBLANES - 2:SUBLANES - 2 + tm, :]
                carry_scr[c, idx] = u[tm - SUBLANES:]
                u_ref[:, cols] = u[tm - SUBLANES:]
            conv.append(u2 * cw_ref[0:1, cols] + u1 * cw_ref[1:2, cols] + u * cw_ref[2:3, cols]
                        + cb_ref[:, cols])
        act_scr[:, c * FFN_TF:(c + 1) * FFN_TF] = (conv[0] * _sigmoid(conv[0]) * conv[1]).astype(BF16)

    y_ref[...] = x1 + _rms(_mm(act_scr[...], wd_ref[...]), gpost_ref[...])


def _mlp(x2d, rw, att, w, tm, seq_len, state_rows=None):
    n = x2d.shape[0]
    has_state = state_rows is not None
    tiles_per_seq = max(seq_len // tm, 1)
    row = lambda i: (i, 0)
    vec = lambda: _const_spec((1, D_MODEL))
    in_specs = [pl.BlockSpec((tm, D_MODEL), row), pl.BlockSpec((tm, RWKV_W), row),
                pl.BlockSpec((tm, ATT_W), row), _const_spec((D_MODEL, D_MODEL)), vec(), vec(), vec(),
                _const_spec((D_MODEL, 2 * D_FF)), _const_spec((CONV_W, 2 * D_FF)),
                _const_spec((1, 2 * D_FF)), _const_spec((D_FF, D_MODEL))]
    args = [x2d, rw, att, w["w_out"], w["norm_mix_post"], w["norm_ffn_pre"], w["norm_ffn_post"],
            w["w_ffn_up"], w["ffn_conv_w"], w["ffn_conv_b"], w["w_ffn_down"]]
    scratch = [pltpu.VMEM((tm, D_FF), BF16)]
    if has_state:
        assert tm % seq_len == 0 and seq_len == SUBLANES
        in_specs.append(pl.BlockSpec((tm, 2 * D_FF), row))
        args.append(state_rows)
        u_spec = pl.BlockSpec((tm, 2 * D_FF), row)
        u_shape = jax.ShapeDtypeStruct((n, 2 * D_FF), F32)
    else:
        assert seq_len % tm == 0
        scratch += [pltpu.VMEM((2, 2, tm + SUBLANES, FFN_TF), F32),
                    pltpu.VMEM((N_FCHUNK, 2, SUBLANES, FFN_TF), F32)]
        u_spec = pl.BlockSpec((None, SUBLANES, 2 * D_FF), lambda i: (i, 0, 0))
        u_shape = jax.ShapeDtypeStruct((n // tm, SUBLANES, 2 * D_FF), F32)
    kern = functools.partial(_mlp_kernel, tm=tm, tiles_per_seq=tiles_per_seq, seq_len=seq_len,
                             has_state=has_state)
    return pl.pallas_call(
        kern,
        grid=(n // tm,),
        in_specs=in_specs,
        out_specs=[pl.BlockSpec((tm, D_MODEL), row), u_spec],
        out_shape=[jax.ShapeDtypeStruct((n, D_MODEL), F32), u_shape],
        scratch_shapes=scratch,
        compiler_params=_params(("arbitrary",)),
        name="mlp",
    )(*args)


def _rope_tables(pos):
    half = HEAD_DIM // 2
    inv = ROPE_THETA ** (-jnp.arange(half, dtype=F32) / half)
    ang = pos.astype(F32)[:, None] * inv[None, :]
    cos, sin = jnp.cos(ang), jnp.sin(ang)
    cos_t = jnp.tile(jnp.concatenate([cos, cos], axis=1), (1, LANES // HEAD_DIM))
    sin_t = jnp.tile(jnp.concatenate([-sin, sin], axis=1), (1, LANES // HEAD_DIM))
    return cos_t, sin_t


def _layer(x, pos, h_prev, wkv0, k_buf, v_buf, conv_prev, w, tm, chunk, mlp_tm):
    b, t, _ = x.shape
    n = b * t
    x2d = x.reshape(n, D_MODEL)
    cos_t, sin_t = _rope_tables(pos)
    if t < tm:
        cos_t = jnp.tile(cos_t, (tm // t, 1))
        sin_t = jnp.tile(sin_t, (tm // t, 1))
        pos_tiles = 1
    else:
        pos_tiles = t // tm
    keep = min(N_STEPS * max(DILATIONS), t)
    tail_tiles = keep // tm if k_buf is None else 0
    rw, q, k, v, *tails = _in_proj(x2d, w["norm_mix_pre"], w["w_in"], cos_t, sin_t, tm, pos_tiles, tail_tiles)
    h_last, pstart = _shift_state(x[:, -1], w["norm_mix_pre"], h_prev, w["w_in_rw"])

    rwkv_out, wkv_new = _rwkv(rw.reshape(b, t, RWKV_COLS), pstart.reshape(b, 1, RWKV_COLS), wkv0,
                              w["rwkv"], chunk)
    q3, k3, v3 = (a.reshape(b, t, ATT_W) for a in (q, k, v))
    if k_buf is None:
        att = _attn_prompt(q3, k3, v3)
        k_new, v_new = (a.reshape(b, keep, ATT_W) for a in tails)
    else:
        w_buf = k_buf.shape[1]
        att, k_new, v_new = _attn_sample(q3, k3, v3, k_buf.reshape(b, w_buf, ATT_W),
                                         v_buf.reshape(b, w_buf, ATT_W))
    rw2d, att2d = rwkv_out.reshape(n, RWKV_W), att.reshape(n, ATT_W)
    if conv_prev is None:
        y, tails = _mlp(x2d, rw2d, att2d, w, mlp_tm, t)
        conv_new = tails.reshape(b, t // mlp_tm, SUBLANES, 2 * D_FF)[:, -1, SUBLANES - (CONV_W - 1):]
    else:
        p2 = jnp.pad(conv_prev, ((0, 0), (0, t - (CONV_W - 1)), (0, 0))).reshape(n, 2 * D_FF)
        y, u = _mlp(x2d, rw2d, att2d, w, mlp_tm, t, state_rows=p2)
        conv_new = u.reshape(b, t, 2 * D_FF)[:, t - (CONV_W - 1):]
    heads = lambda a: a.reshape(b, a.shape[1], N_HEADS, HEAD_DIM)
    return y.reshape(b, t, D_MODEL), h_last, wkv_new, heads(k_new), heads(v_new), conv_new


def _prep_weights(norm_mix_pre, norm_mix_post, norm_ffn_pre, norm_ffn_post, w_in, mu_shift, w0,
                  w_decay_up, a0, w_iclr_up, w_gate_up, k_k, k_a, r_k, lnx_w, lnx_b, w_out,
                  w_ffn_up, ffn_conv_w, ffn_conv_b, w_ffn_down):
    vec = lambda a: a.reshape(1, -1)
    zero = jnp.zeros((D_DECAY, RWKV_W), F32)
    lora = jnp.concatenate([jnp.concatenate([w_decay_up, zero], axis=1),
                            jnp.concatenate([zero, w_iclr_up], axis=1)], axis=0)
    lora_hi, lora_lo = _split(lora)
    wg_hi, wg_lo = _split(w_gate_up)
    w_in_bf = w_in.astype(BF16)
    return {
        "norm_mix_pre": vec(norm_mix_pre), "norm_mix_post": vec(norm_mix_post),
        "norm_ffn_pre": vec(norm_ffn_pre), "norm_ffn_post": vec(norm_ffn_post),
        "w_in": w_in_bf, "w_in_rw": w_in_bf[:, :RWKV_COLS],
        "rwkv": (vec(mu_shift), vec(w0), vec(a0), lora_hi, lora_lo, wg_hi, wg_lo,
                 vec(k_k), vec(k_a), vec(r_k), vec(lnx_w), vec(lnx_b)),
        "w_out": w_out.astype(BF16), "w_ffn_up": w_ffn_up.astype(BF16),
        "ffn_conv_w": ffn_conv_w, "ffn_conv_b": vec(ffn_conv_b), "w_ffn_down": w_ffn_down.astype(BF16),
    }


PROMPT_TM = 512
PROMPT_CHUNK = 64
SAMPLE_MLP_TM = 128


def kernel(x_prompt, x_sample, state_rwkv_shift, state_rwkv_wkv, cache_att_k, cache_att_v, state_ffn_conv, norm_mix_pre, norm_mix_post, norm_ffn_pre, norm_ffn_post, w_in, mu_shift, w0, w_decay_up, a0, w_iclr_up, w_gate_up, k_k, k_a, r_k, lnx_w, lnx_b, w_out, w_ffn_up, ffn_conv_w, ffn_conv_b, w_ffn_down):
    bp, tp, _ = x_prompt.shape
    bs, ts, _ = x_sample.shape
    depth = norm_mix_pre.shape[0]
    pos_p = jnp.arange(tp, dtype=jnp.int32)
    pos_s = PAST_LEN + jnp.arange(ts, dtype=jnp.int32)
    yp, ys = x_prompt, x_sample
    outs_p = [[] for _ in range(5)]
    outs_s = [[] for _ in range(5)]
    for l in range(depth):
        w = _prep_weights(norm_mix_pre[l], norm_mix_post[l], norm_ffn_pre[l], norm_ffn_post[l], w_in[l],
                          mu_shift[l], w0[l], w_decay_up[l], a0[l], w_iclr_up[l], w_gate_up[l], k_k[l],
                          k_a[l], r_k[l], lnx_w[l], lnx_b[l], w_out[l], w_ffn_up[l], ffn_conv_w[l],
                          ffn_conv_b[l], w_ffn_down[l])
        yp, *state_p = _layer(yp, pos_p, jnp.zeros((bp, D_MODEL), F32),
                              jnp.zeros((bp, N_HEADS, HEAD_DIM, HEAD_DIM), F32), None, None, None,
                              w, min(PROMPT_TM, tp), min(PROMPT_CHUNK, tp), min(PROMPT_TM, tp))
        ys, *state_s = _layer(ys, pos_s, state_rwkv_shift[l], state_rwkv_wkv[l], cache_att_k[l],
                              cache_att_v[l], state_ffn_conv[l], w, bs * ts, ts, SAMPLE_MLP_TM)
        for acc, val in zip(outs_p, state_p):
            acc.append(val)
        for acc, val in zip(outs_s, state_s):
            acc.append(val)
    return (yp, ys, *(jnp.stack(a) for a in outs_p), *(jnp.stack(a) for a in outs_s))
```

```python
import functools

import jax
import jax.numpy as jnp
from jax import lax
from jax.experimental import pallas as pl
from jax.experimental.pallas import tpu as pltpu

F32 = jnp.float32
BF16 = jnp.bfloat16

D_MODEL = 1024
HEAD_DIM = 64
RWKV_W = 512
ATT_W = 512
N_HEADS = 8
N_PAIRS = N_HEADS // 2
D_DECAY = 64
D_ICLR = 64
D_GATE = 128
RWKV_COLS = 3 * RWKV_W + D_DECAY + D_ICLR + D_GATE
D_IN = RWKV_COLS + 3 * ATT_W
DILATIONS = (1, 4, 16)
N_STEPS = 128
ROPE_THETA = 10000.0
D_FF = 2816
CONV_W = 3
NORM_EPS = 1e-6
LNX_EPS = 64e-5
NEG_INF = -1e30
PAST_LEN = 16384

LANES = 128
SUBLANES = 8
VMEM_LIMIT = 56 * 1024 * 1024

NN = (((1,), (0,)), ((), ()))
NT = (((1,), (1,)), ((), ()))
TN = (((0,), (0,)), ((), ()))


def _mm(a, b, dims=NN):
    return lax.dot_general(a, b, dims, preferred_element_type=F32)


def _split(x):
    hi = x.astype(BF16)
    lo = (x - hi.astype(F32)).astype(BF16)
    return hi, lo


def _mm1(a, b, dims=NN):
    return _mm(a.astype(BF16), b.astype(BF16), dims)


_mm_gram = _mm1
_mm_inv = _mm1
_mm_apply = _mm1
_mm_state = _mm1


def _mm_split_lhs(a, b_bf16):
    ah, al = _split(a)
    return _mm(ah, b_bf16) + _mm(al, b_bf16)


def _rms(x, g):
    return x * lax.rsqrt(jnp.mean(x * x, axis=-1, keepdims=True) + NORM_EPS) * g


def _sigmoid(x):
    return 1.0 / (1.0 + jnp.exp(-x))


def _params(sem):
    return pltpu.CompilerParams(dimension_semantics=sem, vmem_limit_bytes=VMEM_LIMIT)


def _const_spec(shape):
    nd = len(shape)
    return pl.BlockSpec(shape, lambda *_: (0,) * nd, pipeline_mode=pl.Buffered(1))


def _in_proj_kernel(x_ref, g_ref, w_ref, cos_ref, sin_ref, rw_ref, q_ref, k_ref, v_ref, *tail_refs):
    lane = lax.broadcasted_iota(jnp.int32, (1, ATT_W), 1)
    first_half = (lane % HEAD_DIM) < HEAD_DIM // 2
    q0 = RWKV_COLS
    hb = _rms(x_ref[...], g_ref[...]).astype(BF16)
    rw_ref[...] = _mm(hb, w_ref[:, :RWKV_COLS]).astype(BF16)
    cos = jnp.concatenate([cos_ref[...]] * (ATT_W // LANES), axis=1)
    sin = jnp.concatenate([sin_ref[...]] * (ATT_W // LANES), axis=1)

    def rope(t):
        partner = jnp.where(first_half, pltpu.roll(t, ATT_W - HEAD_DIM // 2, 1),
                            pltpu.roll(t, HEAD_DIM // 2, 1))
        return t * cos + partner * sin

    q_ref[...] = rope(_mm(hb, w_ref[:, q0:q0 + ATT_W]))
    k = rope(_mm(hb, w_ref[:, q0 + ATT_W:q0 + 2 * ATT_W]))
    v = _mm(hb, w_ref[:, q0 + 2 * ATT_W:q0 + 3 * ATT_W])
    k_ref[...] = k
    v_ref[...] = v
    if tail_refs:
        tail_refs[0][...] = k
        tail_refs[1][...] = v


def _in_proj(x2d, g, w_bf, cos_t, sin_t, tm, pos_tiles, tail_tiles=0):
    n = x2d.shape[0]
    row = lambda i: (i, 0)
    pos = lambda i: (i % pos_tiles, 0)
    out_specs = [pl.BlockSpec((tm, RWKV_COLS), row)] + [pl.BlockSpec((tm, ATT_W), row)] * 3
    out_shape = [jax.ShapeDtypeStruct((n, RWKV_COLS), BF16)] + [jax.ShapeDtypeStruct((n, ATT_W), F32)] * 3
    if tail_tiles:
        skip = pos_tiles - tail_tiles
        tail = lambda i: (i // pos_tiles, jnp.maximum(i % pos_tiles - skip, 0), 0)
        out_specs += [pl.BlockSpec((None, tm, ATT_W), tail)] * 2
        out_shape += [jax.ShapeDtypeStruct((n // (pos_tiles * tm), tail_tiles * tm, ATT_W), F32)] * 2
    return pl.pallas_call(
        _in_proj_kernel,
        grid=(n // tm,),
        in_specs=[pl.BlockSpec((tm, D_MODEL), row), _const_spec((1, D_MODEL)),
                  _const_spec((D_MODEL, D_IN)),
                  pl.BlockSpec((tm, LANES), pos), pl.BlockSpec((tm, LANES), pos)],
        out_specs=out_specs,
        out_shape=out_shape,
        compiler_params=_params(("arbitrary",)),
        name="in_proj",
    )(x2d, g, w_bf, cos_t, sin_t)


def _shift_state_kernel(x_ref, g_ref, h_ref, w_ref, hl_ref, ps_ref):
    hl_ref[...] = _rms(x_ref[...], g_ref[...])
    ps_ref[...] = _mm(h_ref[...].astype(BF16), w_ref[...])


def _shift_state(x_last, g, h_prev, w_rw_bf):
    b = x_last.shape[0]
    return pl.pallas_call(
        _shift_state_kernel,
        out_shape=[jax.ShapeDtypeStruct((b, D_MODEL), F32), jax.ShapeDtypeStruct((b, RWKV_COLS), F32)],
        compiler_params=_params(None),
        name="shift_state",
    )(x_last, g, h_prev, w_rw_bf)


def _rwkv_kernel(p_ref, ps_ref, s0_ref, mu_ref, w0_ref, a0_ref, lora_hi_ref, lora_lo_ref,
                 wg_hi_ref, wg_lo_ref, kk_ref, ka_ref, rk_ref, lnw_ref, lnb_ref,
                 y_ref, sfin_ref, s_scr, prev_scr, *, C, n_chunks, nb):
    ci = pl.program_id(1)
    C2 = 2 * C
    rows = nb * C
    seqs = range(nb)
    rsl = [slice(C * s, C * (s + 1)) for s in seqs]

    @pl.when(ci == 0)
    def _():
        prev_scr[...] = ps_ref[...]
        z = jnp.zeros((HEAD_DIM, HEAD_DIM), F32)
        for s in seqs:
            for p in range(N_PAIRS):
                top = jnp.concatenate([s0_ref[s, 2 * p], z], axis=1)
                bot = jnp.concatenate([z, s0_ref[s, 2 * p + 1]], axis=1)
                s_scr[s * N_PAIRS + p] = jnp.concatenate([top, bot], axis=0)

    def per_seq(vals):
        return jnp.concatenate([jnp.broadcast_to(x, (C, x.shape[-1])) for x in vals], axis=0)

    P = p_ref[...].astype(F32).reshape(rows, RWKV_COLS)
    row_c = lax.broadcasted_iota(jnp.int32, (rows, 1), 0) % C
    shifted = jnp.where(row_c == 0, per_seq([prev_scr[s] for s in seqs]), pltpu.roll(P, 1, 0))
    for s in seqs:
        prev_scr[s] = P[C * (s + 1) - 1:C * (s + 1), :]
    mixed = P + (shifted - P) * mu_ref[...]

    r = mixed[:, 0:RWKV_W]
    k = mixed[:, RWKV_W:2 * RWKV_W]
    v = mixed[:, 2 * RWKV_W:3 * RWKV_W]
    xl = mixed[:, 3 * RWKV_W:3 * RWKV_W + LANES]
    gl = mixed[:, 3 * RWKV_W + LANES:RWKV_COLS]

    lane = lax.broadcasted_iota(jnp.int32, (1, LANES), 1)
    lane_lo = lane < HEAD_DIM
    th, tl = _split(jnp.where(lane_lo, jnp.tanh(xl), xl))
    lora = _mm(th, lora_hi_ref[...]) + (_mm(th, lora_lo_ref[...]) + _mm(tl, lora_hi_ref[...]))
    w_pre = w0_ref[...] + lora[:, :RWKV_W]
    a_pre = a0_ref[...] + lora[:, RWKV_W:]
    z = -w_pre
    softplus = jnp.maximum(z, 0.0) + jnp.log1p(jnp.exp(-jnp.abs(z)))
    logd = -jnp.exp(-softplus - 0.5)
    a_lr = _sigmoid(a_pre)
    sgh, sgl = _split(_sigmoid(gl))
    gate = _mm(sgh, wg_hi_ref[...]) + (_mm(sgh, wg_lo_ref[...]) + _mm(sgl, wg_hi_ref[...]))

    ri = lax.broadcasted_iota(jnp.int32, (LANES, LANES), 0)
    cj = lax.broadcasted_iota(jnp.int32, (LANES, LANES), 1)
    seg_ones = jnp.where((ri < HEAD_DIM) == (cj < HEAD_DIM), 1.0, 0.0).astype(BF16)

    def seg_sum(x):
        return jnp.concatenate(
            [_mm_split_lhs(x[:, LANES * p:LANES * (p + 1)], seg_ones) for p in range(N_PAIRS)], axis=1)

    kkr = k * kk_ref[...]
    kkn = kkr / jnp.maximum(jnp.sqrt(seg_sum(kkr * kkr)), 1e-12)
    kmod = k * (1.0 + (a_lr - 1.0) * ka_ref[...])
    bvec = kkn * a_lr
    avec = -kkn

    c = logd
    s = 1
    while s < C:
        c = c + jnp.where(row_c >= s, pltpu.roll(c, s, 0), 0.0)
        s *= 2
    mid = C // 2 - 1
    c_mid = [c[C * s + mid:C * s + mid + 1, :] for s in seqs]
    c_end = [c[C * (s + 1) - 1:C * (s + 1), :] for s in seqs]
    cp = c - per_seq(c_mid)
    e_pos = jnp.exp(cp)
    e_neg = jnp.exp(-cp)
    e_prev = jnp.exp(cp - logd)
    e_mid = [jnp.exp(x) for x in c_mid]
    e_end = [jnp.exp(x) for x in c_end]
    e_endp = [jnp.exp(c_end[s] - c_mid[s]) for s in seqs]
    a_t = avec * e_prev
    r_t = r * e_pos
    b_t = bvec * e_neg
    k_t = kmod * e_neg
    bonus_in = r * kmod * rk_ref[...]

    def stack(x):
        return jnp.concatenate([jnp.where(lane_lo, x, 0.0), jnp.where(lane_lo, 0.0, x)], axis=0)

    ri2 = lax.broadcasted_iota(jnp.int32, (C2, C2), 0)
    cj2 = lax.broadcasted_iota(jnp.int32, (C2, C2), 1)
    strict = cj2 < ri2
    incl = cj2 <= ri2
    n_sq = C.bit_length() - 2
    fused = C2 % LANES == 0
    unit_seq = [s for s in seqs for _ in range(N_PAIRS)]
    pairs = range(nb * N_PAIRS)
    sls = [slice(LANES * p, LANES * (p + 1)) for _ in seqs for p in range(N_PAIRS)]
    rws = [rsl[s] for s in unit_seq]
    e_mid = [e_mid[s] for s in unit_seq]
    e_end = [e_end[s] for s in unit_seq]
    e_endp = [e_endp[s] for s in unit_seq]

    ar = [jnp.concatenate([stack(a_t[rws[p], sls[p]]), stack(r_t[rws[p], sls[p]])], axis=0)
          for p in pairs]
    bk = [jnp.concatenate([stack(b_t[rws[p], sls[p]]), stack(k_t[rws[p], sls[p]])], axis=0) for p in pairs]
    v_st = [stack(v[rws[p], sls[p]]) for p in pairs]
    if fused:
        gram = [_mm_gram(ar[p], bk[p], NT) for p in pairs]
        g_ab = [jnp.where(strict, g[:C2, :C2], 0.0) for g in gram]
        g_ak = [jnp.where(strict, g[:C2, C2:], 0.0) for g in gram]
        g_rb = [jnp.where(incl, g[C2:, :C2], 0.0) for g in gram]
        g_rk = [jnp.where(incl, g[C2:, C2:], 0.0) for g in gram]
    else:
        g_ab = [jnp.where(strict, _mm_gram(ar[p][:C2], bk[p][:C2], NT), 0.0) for p in pairs]
        g_ak = [jnp.where(strict, _mm_gram(ar[p][:C2], bk[p][C2:], NT), 0.0) for p in pairs]
        g_rb = [jnp.where(incl, _mm_gram(ar[p][C2:], bk[p][:C2], NT), 0.0) for p in pairs]
        g_rk = [jnp.where(incl, _mm_gram(ar[p][C2:], bk[p][C2:], NT), 0.0) for p in pairs]

    z2 = [_mm_apply(g_ak[p], v_st[p]) for p in pairs]
    x = [jnp.concatenate([ar[p][:C2] * e_mid[p][:, sls[p]], z2[p]], axis=1) for p in pairs]
    lp = g_ab
    x = [x[p] + _mm_apply(lp[p], x[p]) for p in pairs]
    for _ in range(n_sq):
        lp = [_mm_inv(lp[p], lp[p]) for p in pairs]
        x = [x[p] + _mm_apply(lp[p], x[p]) for p in pairs]

    s_old = [s_scr[p] for p in pairs]
    wr = [jnp.concatenate([x[p][:, :LANES], ar[p][C2:] * e_mid[p][:, sls[p]]], axis=0) for p in pairs]
    ws = [_mm_state(wr[p], s_old[p], NT) for p in pairs]
    u = [ws[p][:C2] + x[p][:, LANES:] for p in pairs]
    uv = [jnp.concatenate([u[p], v_st[p]], axis=0) for p in pairs]
    if fused:
        y_st = [ws[p][C2:] + _mm_state(jnp.concatenate([g_rb[p], g_rk[p]], axis=1), uv[p]) for p in pairs]
    else:
        y_st = [ws[p][C2:] + _mm_state(g_rb[p], u[p]) + _mm_state(g_rk[p], v_st[p]) for p in pairs]
    for p in pairs:
        s_scr[p] = s_old[p] * e_end[p][:, sls[p]] + _mm_state(uv[p], bk[p] * e_endp[p][:, sls[p]], TN)

    y = [jnp.concatenate([y_st[s * N_PAIRS + p][:C] + y_st[s * N_PAIRS + p][C:] for s in seqs], axis=0)
         for p in range(N_PAIRS)]
    for p in range(N_PAIRS):
        sl = sls[p]
        dev = y[p] - _mm_split_lhs(y[p], seg_ones) * (1.0 / HEAD_DIM)
        var = _mm_split_lhs(dev * dev, seg_ones) * (1.0 / HEAD_DIM)
        yn = dev * lax.rsqrt(var + LNX_EPS) * lnw_ref[:, sl] + lnb_ref[:, sl]
        bonus = _mm_split_lhs(bonus_in[:, sl], seg_ones) * v[:, sl]
        y_ref[:, :, sl] = ((yn + bonus) * gate[:, sl]).reshape(nb, C, LANES)

    @pl.when(ci == n_chunks - 1)
    def _():
        for s in seqs:
            for p in range(N_PAIRS):
                s_fin = s_scr[s * N_PAIRS + p]
                sfin_ref[s, 2 * p] = s_fin[:HEAD_DIM, :HEAD_DIM]
                sfin_ref[s, 2 * p + 1] = s_fin[HEAD_DIM:, HEAD_DIM:]


RWKV_SEQS = 8


def _rwkv(proj_rw, pstart, s0, wts, C):
    b, t, _ = proj_rw.shape
    nb = RWKV_SEQS
    assert b % nb == 0 and t % C == 0
    n_chunks = t // C
    vec = lambda n: _const_spec((1, n))
    kern = functools.partial(_rwkv_kernel, C=C, n_chunks=n_chunks, nb=nb)
    return pl.pallas_call(
        kern,
        grid=(b // nb, n_chunks),
        in_specs=[pl.BlockSpec((nb, C, RWKV_COLS), lambda i, j: (i, j, 0)),
                  pl.BlockSpec((nb, 1, RWKV_COLS), lambda i, j: (i, 0, 0)),
                  pl.BlockSpec((nb, N_HEADS, HEAD_DIM, HEAD_DIM), lambda i, j: (i, 0, 0, 0)),
                  vec(RWKV_COLS), vec(RWKV_W), vec(RWKV_W),
                  _const_spec((LANES, 2 * RWKV_W)), _const_spec((LANES, 2 * RWKV_W)),
                  _const_spec((D_GATE, RWKV_W)), _const_spec((D_GATE, RWKV_W)),
                  vec(RWKV_W), vec(RWKV_W), vec(RWKV_W), vec(RWKV_W), vec(RWKV_W)],
        out_specs=[pl.BlockSpec((nb, C, RWKV_W), lambda i, j: (i, j, 0)),
                   pl.BlockSpec((nb, N_HEADS, HEAD_DIM, HEAD_DIM), lambda i, j: (i, 0, 0, 0))],
        out_shape=[jax.ShapeDtypeStruct((b, t, RWKV_W), F32),
                   jax.ShapeDtypeStruct((b, N_HEADS, HEAD_DIM, HEAD_DIM), F32)],
        scratch_shapes=[pltpu.VMEM((nb * N_PAIRS, LANES, LANES), F32), pltpu.VMEM((nb, 1, RWKV_COLS), F32)],
        compiler_params=_params(("parallel", "arbitrary")),
        name="rwkv",
    )(proj_rw, pstart, s0, *wts)


ATT_UNROLL = 4
LOG2E = 1.4426950408889634


def _attn_prompt_kernel(q_ref, k_ref, v_ref, bias_ref, o_ref, acc_scr, m_scr, l_scr, *, T):
    QB = N_STEPS
    lane = lax.broadcasted_iota(jnp.int32, (1, LANES), 1)
    lane_lo = lane < HEAD_DIM

    def halves(x):
        return jnp.where(lane_lo, x[:QB], x[QB:])

    order = sorted(DILATIONS, reverse=True)
    for bi, d in enumerate(order):
        per_res = T // (QB * d)
        ub = min(ATT_UNROLL, per_res)
        groups = ATT_UNROLL // ub
        gpr = per_res // ub

        def rows(start, d=d):
            if d == 1:
                return pl.ds(pl.multiple_of(start, QB), QB)
            return pl.ds(start, QB, stride=d)

        def body(it, carry, bi=bi, d=d, ub=ub, groups=groups, gpr=gpr, rows=rows):
            blocks = []
            for g in range(groups):
                gi = it * groups + g
                res = gi // gpr
                n0 = (gi % gpr) * ub
                starts = [res + d * QB * jnp.maximum(n0 - 1, 0)] + [res + d * QB * (n0 + u) for u in range(ub)]
                kb = [k_ref[rows(s), :].astype(BF16) for s in starts]
                vb = [v_ref[rows(s), :].astype(BF16) for s in starts]
                for u in range(ub):
                    bias = bias_ref[jnp.where(n0 == 0, 1, 0)] if u == 0 else bias_ref[0]
                    blocks.append((starts[u + 1], jnp.concatenate([kb[u], kb[u + 1]], axis=0),
                                   jnp.concatenate([vb[u], vb[u + 1]], axis=0), bias))
            q_st = []
            for start, _, _, _ in blocks:
                q = q_ref[rows(start), :] * (HEAD_DIM ** -0.5 * LOG2E)
                q_st.append(jnp.concatenate([jnp.where(lane_lo, q, 0.0), jnp.where(lane_lo, 0.0, q)],
                                            axis=0).astype(BF16))
            s = [_mm(q_st[i], blk[1], NT) + blk[3] for i, blk in enumerate(blocks)]
            m = [jnp.max(x, axis=1, keepdims=True) for x in s]
            p = [jnp.exp2(s[i] - m[i]) for i in range(len(blocks))]
            l = [jnp.sum(x, axis=1, keepdims=True) for x in p]
            o = [halves(_mm(p[i].astype(BF16), blk[2])) for i, blk in enumerate(blocks)]
            m_b = [halves(jnp.broadcast_to(x, (2 * QB, LANES))) for x in m]
            l_b = [halves(jnp.broadcast_to(x, (2 * QB, LANES))) for x in l]
            if bi == 0:
                for i, blk in enumerate(blocks):
                    acc_scr[rows(blk[0]), :] = o[i]
                    m_scr[rows(blk[0]), :] = m_b[i]
                    l_scr[rows(blk[0]), :] = l_b[i]
            else:
                m_old = [m_scr[rows(blk[0]), :] for blk in blocks]
                acc_old = [acc_scr[rows(blk[0]), :] for blk in blocks]
                l_old = [l_scr[rows(blk[0]), :] for blk in blocks]
                for i, blk in enumerate(blocks):
                    m_new = jnp.maximum(m_old[i], m_b[i])
                    w_old = jnp.exp2(m_old[i] - m_new)
                    w_new = jnp.exp2(m_b[i] - m_new)
                    acc = acc_old[i] * w_old + o[i] * w_new
                    den = l_old[i] * w_old + l_b[i] * w_new
                    if bi < len(order) - 1:
                        acc_scr[rows(blk[0]), :] = acc
                        m_scr[rows(blk[0]), :] = m_new
                        l_scr[rows(blk[0]), :] = den
                    else:
                        o_ref[rows(blk[0]), :] = acc / den
            return carry

        lax.fori_loop(0, d * gpr // groups, body, 0)


def _band_bias():
    qi = jnp.arange(2 * N_STEPS)[:, None] % N_STEPS
    kj = jnp.arange(2 * N_STEPS)[None, :]
    band = (kj >= qi) & (kj <= qi + N_STEPS)
    normal = jnp.where(band, 0.0, NEG_INF).astype(F32)
    first = jnp.where(band & (kj >= N_STEPS), 0.0, NEG_INF).astype(F32)
    return jnp.stack([normal, first])


def _attn_prompt(q, k, v):
    b, t, _ = q.shape
    assert t % (N_STEPS * max(DILATIONS)) == 0
    spec = pl.BlockSpec((None, t, LANES), lambda i, p: (i, 0, p))
    return pl.pallas_call(
        functools.partial(_attn_prompt_kernel, T=t),
        grid=(b, N_PAIRS),
        in_specs=[spec, spec, spec, _const_spec((2, 2 * N_STEPS, 2 * N_STEPS))],
        out_specs=spec,
        out_shape=jax.ShapeDtypeStruct((b, t, ATT_W), F32),
        scratch_shapes=[pltpu.VMEM((t, LANES), F32)] * 3,
        compiler_params=_params(("parallel", "parallel")),
        name="attn_prompt",
    )(q, k, v, _band_bias())


def _attn_sample_kernel(q_ref, kn_ref, vn_ref, kc_ref, vc_ref, cntc_ref, cntn_ref,
                        o_ref, ko_ref, vo_ref, *, ts, w_buf):
    rows = N_HEADS * ts
    ri = lax.broadcasted_iota(jnp.int32, (rows, ATT_W), 0)
    cj = lax.broadcasted_iota(jnp.int32, (rows, ATT_W), 1)
    own = (ri // ts) == (cj // HEAD_DIM)
    q_all = jnp.concatenate([q_ref[...] * (HEAD_DIM ** -0.5)] * N_HEADS, axis=0)
    q_st = jnp.where(own, q_all, 0.0).astype(BF16)
    kc = kc_ref[...]
    vc = vc_ref[...]
    kn = kn_ref[...]
    vn = vn_ref[...]
    cnt_c = cntc_ref[...]
    cnt_n = cntn_ref[...]
    s_c = jnp.where(cnt_c > 0.0, _mm(q_st, kc.astype(BF16), NT), NEG_INF)
    s_n = jnp.where(cnt_n > 0.0, _mm(q_st, kn.astype(BF16), NT), NEG_INF)
    m = jnp.maximum(jnp.max(s_c, axis=1, keepdims=True), jnp.max(s_n, axis=1, keepdims=True))
    p_c = cnt_c * jnp.exp(s_c - m)
    p_n = cnt_n * jnp.exp(s_n - m)
    den = jnp.sum(p_c, axis=1, keepdims=True) + jnp.sum(p_n, axis=1, keepdims=True)
    o_st = (_mm(p_c.astype(BF16), vc.astype(BF16)) + _mm(p_n.astype(BF16), vn.astype(BF16))) / den
    o_st = jnp.where(own, o_st, 0.0)
    out = o_st[0:ts]
    for h in range(1, N_HEADS):
        out = out + o_st[h * ts:(h + 1) * ts]
    o_ref[...] = out
    ko_ref[0:w_buf - ts, :] = kc[ts:w_buf]
    ko_ref[w_buf - ts:w_buf, :] = kn
    vo_ref[0:w_buf - ts, :] = vc[ts:w_buf]
    vo_ref[w_buf - ts:w_buf, :] = vn


def _branch_counts(ts, w_buf):
    delta = (w_buf + jnp.arange(ts)[:, None]) - jnp.arange(w_buf + ts)[None, :]
    cnt = jnp.zeros(delta.shape, F32)
    for d in DILATIONS:
        cnt = cnt + ((delta >= 0) & (delta % d == 0) & (delta // d <= N_STEPS)).astype(F32)
    cnt = jnp.tile(cnt, (N_HEADS, 1))
    return cnt[:, :w_buf], cnt[:, w_buf:]


def _attn_sample(q, kn, vn, kcache, vcache):
    b, ts, _ = q.shape
    w_buf = kcache.shape[1]
    assert w_buf >= N_STEPS * max(DILATIONS) and ts % SUBLANES == 0
    cnt_c, cnt_n = _branch_counts(ts, w_buf)
    new = pl.BlockSpec((None, ts, ATT_W), lambda i: (i, 0, 0))
    cache = pl.BlockSpec((None, w_buf, ATT_W), lambda i: (i, 0, 0))
    return pl.pallas_call(
        functools.partial(_attn_sample_kernel, ts=ts, w_buf=w_buf),
        grid=(b,),
        in_specs=[new, new, new, cache, cache,
                  _const_spec((N_HEADS * ts, w_buf)), _const_spec((N_HEADS * ts, ts))],
        out_specs=[new, cache, cache],
        out_shape=[jax.ShapeDtypeStruct((b, ts, ATT_W), F32),
                   jax.ShapeDtypeStruct((b, w_buf, ATT_W), F32),
                   jax.ShapeDtypeStruct((b, w_buf, ATT_W), F32)],
        compiler_params=_params(("parallel",)),
        name="attn_sample",
    )(q, kn, vn, kcache, vcache, cnt_c, cnt_n)


FFN_TF = 256
N_FCHUNK = D_FF // FFN_TF


def _mlp_kernel(*refs, tm, tiles_per_seq, seq_len, has_state):
    if has_state:
        (x_ref, rw_ref, att_ref, wout_ref, gmix_ref, gpre_ref, gpost_ref, wup_ref, cw_ref, cb_ref,
         wd_ref, p2_ref, y_ref, u_ref, act_scr) = refs
    else:
        (x_ref, rw_ref, att_ref, wout_ref, gmix_ref, gpre_ref, gpost_ref, wup_ref, cw_ref, cb_ref,
         wd_ref, y_ref, u_ref, act_scr, ubuf_scr, carry_scr) = refs

        @pl.when((pl.program_id(0) % tiles_per_seq) == 0)
        def _():
            carry_scr[...] = jnp.zeros_like(carry_scr)

    mix = (_mm(rw_ref[...].astype(BF16), wout_ref[:RWKV_W, :])
           + _mm(att_ref[...].astype(BF16), wout_ref[RWKV_W:, :]))
    x1 = x_ref[...] + _rms(mix, gmix_ref[...])
    hb = _rms(x1, gpre_ref[...]).astype(BF16)
    t_in = lax.broadcasted_iota(jnp.int32, (tm, 1), 0) % seq_len

    for c in range(N_FCHUNK):
        conv = []
        for idx, col0 in enumerate((c * FFN_TF, D_FF + c * FFN_TF)):
            cols = slice(col0, col0 + FFN_TF)
            u = _mm(hb, wup_ref[:, cols])
            if has_state:
                p2 = p2_ref[:, cols]
                u1 = jnp.where(t_in >= 1, pltpu.roll(u, 1, 0), pltpu.roll(p2, tm - 1, 0))
                u2 = jnp.where(t_in >= 2, pltpu.roll(u, 2, 0), p2)
                u_ref[:, cols] = u
            else:
                stage = ubuf_scr.at[c % 2, idx]
                stage[0:SUBLANES, :] = carry_scr[c, idx]
                stage[SUBLANES:, :] = u
                u1 = stage[SUBLANES - 1:SUBLANES - 1 + tm, :]
                u2 = stage[SUBLANES - 2:SUBLANES - 2 + tm, :]
                carry_scr[c, idx] = u[tm - SUBLANES:]
                u_ref[:, cols] = u[tm - SUBLANES:]
            conv.append(u2 * cw_ref[0:1, cols] + u1 * cw_ref[1:2, cols] + u * cw_ref[2:3, cols]
                        + cb_ref[:, cols])
        act_scr[:, c * FFN_TF:(c + 1) * FFN_TF] = (conv[0] * _sigmoid(conv[0]) * conv[1]).astype(BF16)

    y_ref[...] = x1 + _rms(_mm(act_scr[...], wd_ref[...]), gpost_ref[...])


def _mlp(x2d, rw, att, w, tm, seq_len, state_rows=None):
    n = x2d.shape[0]
    has_state = state_rows is not None
    tiles_per_seq = max(seq_len // tm, 1)
    n_tiles = n // tm
    row = lambda i: (i, 0)
    vec = lambda: _const_spec((1, D_MODEL))
    in_specs = [pl.BlockSpec((tm, D_MODEL), row), pl.BlockSpec((tm, RWKV_W), row),
                pl.BlockSpec((tm, ATT_W), row), _const_spec((D_MODEL, D_MODEL)), vec(), vec(), vec(),
                _const_spec((D_MODEL, 2 * D_FF)), _const_spec((CONV_W, 2 * D_FF)),
                _const_spec((1, 2 * D_FF)), _const_spec((D_FF, D_MODEL))]
    args = [x2d, rw, att, w["w_out"], w["norm_mix_post"], w["norm_ffn_pre"], w["norm_ffn_post"],
            w["w_ffn_up"], w["ffn_conv_w"], w["ffn_conv_b"], w["w_ffn_down"]]
    scratch = [pltpu.VMEM((tm, D_FF), BF16)]
    if has_state:
        assert tm % seq_len == 0 and seq_len == SUBLANES
        in_specs.append(pl.BlockSpec((tm, 2 * D_FF), row))
        args.append(state_rows)
        u_spec = pl.BlockSpec((tm, 2 * D_FF), row)
        u_shape = jax.ShapeDtypeStruct((n, 2 * D_FF), F32)
    else:
        assert seq_len % tm == 0
        scratch += [pltpu.VMEM((2, 2, tm + SUBLANES, FFN_TF), F32),
                    pltpu.VMEM((N_FCHUNK, 2, SUBLANES, FFN_TF), F32)]
        u_spec = pl.BlockSpec((None, SUBLANES, 2 * D_FF), lambda i: (i, 0, 0))
        u_shape = jax.ShapeDtypeStruct((n_tiles, SUBLANES, 2 * D_FF), F32)
    kern = functools.partial(_mlp_kernel, tm=tm, tiles_per_seq=tiles_per_seq, seq_len=seq_len,
                             has_state=has_state)
    return pl.pallas_call(
        kern,
        grid=(n_tiles,),
        in_specs=in_specs,
        out_specs=[pl.BlockSpec((tm, D_MODEL), row), u_spec],
        out_shape=[jax.ShapeDtypeStruct((n, D_MODEL), F32), u_shape],
        scratch_shapes=scratch,
        compiler_params=_params(("arbitrary",)),
        name="mlp",
    )(*args)


def _rope_tables(pos):
    half = HEAD_DIM // 2
    inv = ROPE_THETA ** (-jnp.arange(half, dtype=F32) / half)
    ang = pos.astype(F32)[:, None] * inv[None, :]
    cos, sin = jnp.cos(ang), jnp.sin(ang)
    cos_t = jnp.tile(jnp.concatenate([cos, cos], axis=1), (1, LANES // HEAD_DIM))
    sin_t = jnp.tile(jnp.concatenate([-sin, sin], axis=1), (1, LANES // HEAD_DIM))
    return cos_t, sin_t


def _layer(x, pos, h_prev, wkv0, k_buf, v_buf, conv_prev, w, tm, chunk, mlp_tm):
    b, t, _ = x.shape
    n = b * t
    x2d = x.reshape(n, D_MODEL)
    cos_t, sin_t = _rope_tables(pos)
    if t < tm:
        cos_t = jnp.tile(cos_t, (tm // t, 1))
        sin_t = jnp.tile(sin_t, (tm // t, 1))
        pos_tiles = 1
    else:
        pos_tiles = t // tm
    keep = min(N_STEPS * max(DILATIONS), t)
    tail_tiles = keep // tm if k_buf is None else 0
    rw, q, k, v, *tails = _in_proj(x2d, w["norm_mix_pre"], w["w_in"], cos_t, sin_t, tm, pos_tiles, tail_tiles)
    h_last, pstart = _shift_state(x[:, -1], w["norm_mix_pre"], h_prev, w["w_in_rw"])

    rwkv_out, wkv_new = _rwkv(rw.reshape(b, t, RWKV_COLS), pstart.reshape(b, 1, RWKV_COLS), wkv0,
                              w["rwkv"], chunk)
    q3, k3, v3 = (a.reshape(b, t, ATT_W) for a in (q, k, v))
    if k_buf is None:
        att = _attn_prompt(q3, k3, v3)
        k_new, v_new = tails
    else:
        w_buf = k_buf.shape[1]
        att, k_new, v_new = _attn_sample(q3, k3, v3, k_buf.reshape(b, w_buf, ATT_W),
                                         v_buf.reshape(b, w_buf, ATT_W))
    rw2d, att2d = rwkv_out.reshape(n, RWKV_W), att.reshape(n, ATT_W)
    if conv_prev is None:
        y, tails = _mlp(x2d, rw2d, att2d, w, mlp_tm, t)
        conv_new = tails.reshape(b, t // mlp_tm, SUBLANES, 2 * D_FF)[:, -1, SUBLANES - (CONV_W - 1):]
    else:
        p2 = jnp.pad(conv_prev, ((0, 0), (0, t - (CONV_W - 1)), (0, 0))).reshape(n, 2 * D_FF)
        y, u = _mlp(x2d, rw2d, att2d, w, mlp_tm, t, state_rows=p2)
        conv_new = u.reshape(b, t, 2 * D_FF)[:, t - (CONV_W - 1):]
    heads = lambda a: a.reshape(b, a.shape[1], N_HEADS, HEAD_DIM)
    return y.reshape(b, t, D_MODEL), h_last, wkv_new, heads(k_new), heads(v_new), conv_new


def _prep_weights(norm_mix_pre, norm_mix_post, norm_ffn_pre, norm_ffn_post, w_in, mu_shift, w0,
                  w_decay_up, a0, w_iclr_up, w_gate_up, k_k, k_a, r_k, lnx_w, lnx_b, w_out,
                  w_ffn_up, ffn_conv_w, ffn_conv_b, w_ffn_down):
    vec = lambda a: a.reshape(1, -1)
    zero = jnp.zeros((D_DECAY, RWKV_W), F32)
    lora = jnp.concatenate([jnp.concatenate([w_decay_up, zero], axis=1),
                            jnp.concatenate([zero, w_iclr_up], axis=1)], axis=0)
    lora_hi, lora_lo = _split(lora)
    wg_hi, wg_lo = _split(w_gate_up)
    w_in_bf = w_in.astype(BF16)
    return {
        "norm_mix_pre": vec(norm_mix_pre), "norm_mix_post": vec(norm_mix_post),
        "norm_ffn_pre": vec(norm_ffn_pre), "norm_ffn_post": vec(norm_ffn_post),
        "w_in": w_in_bf, "w_in_rw": w_in_bf[:, :RWKV_COLS],
        "rwkv": (vec(mu_shift), vec(w0), vec(a0), lora_hi, lora_lo, wg_hi, wg_lo,
                 vec(k_k), vec(k_a), vec(r_k), vec(lnx_w), vec(lnx_b)),
        "w_out": w_out.astype(BF16), "w_ffn_up": w_ffn_up.astype(BF16),
        "ffn_conv_w": ffn_conv_w, "ffn_conv_b": vec(ffn_conv_b), "w_ffn_down": w_ffn_down.astype(BF16),
    }


PROMPT_TM = 512
PROMPT_CHUNK = 64
SAMPLE_MLP_TM = 128


def kernel(x_prompt, x_sample, state_rwkv_shift, state_rwkv_wkv, cache_att_k, cache_att_v, state_ffn_conv, norm_mix_pre, norm_mix_post, norm_ffn_pre, norm_ffn_post, w_in, mu_shift, w0, w_decay_up, a0, w_iclr_up, w_gate_up, k_k, k_a, r_k, lnx_w, lnx_b, w_out, w_ffn_up, ffn_conv_w, ffn_conv_b, w_ffn_down):
    bp, tp, _ = x_prompt.shape
    bs, ts, _ = x_sample.shape
    depth = norm_mix_pre.shape[0]
    pos_p = jnp.arange(tp, dtype=jnp.int32)
    pos_s = PAST_LEN + jnp.arange(ts, dtype=jnp.int32)
    yp, ys = x_prompt, x_sample
    outs_p = [[] for _ in range(5)]
    outs_s = [[] for _ in range(5)]
    for l in range(depth):
        w = _prep_weights(norm_mix_pre[l], norm_mix_post[l], norm_ffn_pre[l], norm_ffn_post[l], w_in[l],
                          mu_shift[l], w0[l], w_decay_up[l], a0[l], w_iclr_up[l], w_gate_up[l], k_k[l],
                          k_a[l], r_k[l], lnx_w[l], lnx_b[l], w_out[l], w_ffn_up[l], ffn_conv_w[l],
                          ffn_conv_b[l], w_ffn_down[l])
        yp, *state_p = _layer(yp, pos_p, jnp.zeros((bp, D_MODEL), F32),
                              jnp.zeros((bp, N_HEADS, HEAD_DIM, HEAD_DIM), F32), None, None, None,
                              w, min(PROMPT_TM, tp), min(PROMPT_CHUNK, tp), min(PROMPT_TM, tp))
        ys, *state_s = _layer(ys, pos_s, state_rwkv_shift[l], state_rwkv_wkv[l], cache_att_k[l],
                              cache_att_v[l], state_ffn_conv[l], w, bs * ts, ts, SAMPLE_MLP_TM)
        for acc, val in zip(outs_p, state_p):
            acc.append(val)
        for acc, val in zip(outs_s, state_s):
            acc.append(val)
    return (yp, ys, *(jnp.stack(a) for a in outs_p), *(jnp.stack(a) for a in outs_s))
```

```python
import functools

import jax
import jax.numpy as jnp
from jax import lax
from jax.experimental import pallas as pl
from jax.experimental.pallas import tpu as pltpu

F32 = jnp.float32
BF16 = jnp.bfloat16

D_MODEL = 1024
HEAD_DIM = 64
RWKV_W = 512
ATT_W = 512
N_HEADS = 8
N_PAIRS = N_HEADS // 2
D_DECAY = 64
D_ICLR = 64
D_GATE = 128
RWKV_COLS = 3 * RWKV_W + D_DECAY + D_ICLR + D_GATE
D_IN = RWKV_COLS + 3 * ATT_W
DILATIONS = (1, 4, 16)
N_STEPS = 128
ROPE_THETA = 10000.0
D_FF = 2816
CONV_W = 3
NORM_EPS = 1e-6
LNX_EPS = 64e-5
NEG_INF = -1e30
PAST_LEN = 16384

LANES = 128
SUBLANES = 8
VMEM_LIMIT = 56 * 1024 * 1024

NN = (((1,), (0,)), ((), ()))
NT = (((1,), (1,)), ((), ()))
TN = (((0,), (0,)), ((), ()))


def _mm(a, b, dims=NN):
    return lax.dot_general(a, b, dims, preferred_element_type=F32)


def _split(x):
    hi = x.astype(BF16)
    lo = (x - hi.astype(F32)).astype(BF16)
    return hi, lo


def _mm1(a, b, dims=NN):
    return _mm(a.astype(BF16), b.astype(BF16), dims)


_mm_gram = _mm1
_mm_inv = _mm1
_mm_apply = _mm1
_mm_state = _mm1


def _mm_split_lhs(a, b_bf16):
    ah, al = _split(a)
    return _mm(ah, b_bf16) + _mm(al, b_bf16)


def _mm_split_lhs_rhs(a_bf16, b):
    bh, bl = _split(b)
    return _mm(a_bf16, bh) + _mm(a_bf16, bl)


def _rms(x, g):
    return x * lax.rsqrt(jnp.mean(x * x, axis=-1, keepdims=True) + NORM_EPS) * g


def _sigmoid(x):
    return 1.0 / (1.0 + jnp.exp(-x))


def _sigmoid_tanh(x):
    return 0.5 + 0.5 * jnp.tanh(0.5 * x)


EXP_M_HALF = 0.6065306597126334


def _params(sem):
    return pltpu.CompilerParams(dimension_semantics=sem, vmem_limit_bytes=VMEM_LIMIT)


def _const_spec(shape):
    nd = len(shape)
    return pl.BlockSpec(shape, lambda *_: (0,) * nd, pipeline_mode=pl.Buffered(1))


def _in_proj_kernel(x_ref, g_ref, w_ref, cos_ref, sin_ref, rw_ref, q_ref, k_ref, v_ref, *tail_refs):
    lane = lax.broadcasted_iota(jnp.int32, (1, ATT_W), 1)
    first_half = (lane % HEAD_DIM) < HEAD_DIM // 2
    q0 = RWKV_COLS
    hb = _rms(x_ref[...], g_ref[...]).astype(BF16)
    rw_ref[...] = _mm(hb, w_ref[:, :RWKV_COLS]).astype(BF16)
    cos = jnp.concatenate([cos_ref[...]] * (ATT_W // LANES), axis=1)
    sin = jnp.concatenate([sin_ref[...]] * (ATT_W // LANES), axis=1)

    def rope(t):
        partner = jnp.where(first_half, pltpu.roll(t, ATT_W - HEAD_DIM // 2, 1),
                            pltpu.roll(t, HEAD_DIM // 2, 1))
        return t * cos + partner * sin

    q_ref[...] = rope(_mm(hb, w_ref[:, q0:q0 + ATT_W]))
    k = rope(_mm(hb, w_ref[:, q0 + ATT_W:q0 + 2 * ATT_W]))
    v = _mm(hb, w_ref[:, q0 + 2 * ATT_W:q0 + 3 * ATT_W])
    k_ref[...] = k
    v_ref[...] = v
    if tail_refs:
        tail_refs[0][...] = k
        tail_refs[1][...] = v


def _in_proj(x2d, g, w_bf, cos_t, sin_t, tm, pos_tiles, tail_tiles=0):
    n = x2d.shape[0]
    row = lambda i: (i, 0)
    pos = lambda i: (i % pos_tiles, 0)
    out_specs = [pl.BlockSpec((tm, RWKV_COLS), row)] + [pl.BlockSpec((tm, ATT_W), row)] * 3
    out_shape = [jax.ShapeDtypeStruct((n, RWKV_COLS), BF16)] + [jax.ShapeDtypeStruct((n, ATT_W), F32)] * 3
    if tail_tiles:
        skip = pos_tiles - tail_tiles
        tail = lambda i: (i // pos_tiles, jnp.maximum(i % pos_tiles - skip, 0), 0)
        out_specs += [pl.BlockSpec((None, tm, ATT_W), tail)] * 2
        out_shape += [jax.ShapeDtypeStruct((n // (pos_tiles * tm), tail_tiles * tm, ATT_W), F32)] * 2
    return pl.pallas_call(
        _in_proj_kernel,
        grid=(n // tm,),
        in_specs=[pl.BlockSpec((tm, D_MODEL), row), _const_spec((1, D_MODEL)),
                  _const_spec((D_MODEL, D_IN)),
                  pl.BlockSpec((tm, LANES), pos), pl.BlockSpec((tm, LANES), pos)],
        out_specs=out_specs,
        out_shape=out_shape,
        compiler_params=_params(("arbitrary",)),
        name="in_proj",
    )(x2d, g, w_bf, cos_t, sin_t)


def _shift_state_kernel(x_ref, g_ref, h_ref, w_ref, hl_ref, ps_ref):
    hl_ref[...] = _rms(x_ref[...], g_ref[...])
    ps_ref[...] = _mm(h_ref[...].astype(BF16), w_ref[...])


def _shift_state(x_last, g, h_prev, w_rw_bf):
    b = x_last.shape[0]
    return pl.pallas_call(
        _shift_state_kernel,
        out_shape=[jax.ShapeDtypeStruct((b, D_MODEL), F32), jax.ShapeDtypeStruct((b, RWKV_COLS), F32)],
        compiler_params=_params(None),
        name="shift_state",
    )(x_last, g, h_prev, w_rw_bf)


def _rwkv_kernel(p_ref, ps_ref, s0_ref, mu_ref, w0_ref, a0_ref, lora_hi_ref, lora_lo_ref,
                 wg_hi_ref, wg_lo_ref, kk_ref, ka_ref, rk_ref, lnw_ref, lnb_ref,
                 y_ref, sfin_ref, s_scr, prev_scr, pre_scr, bonus_scr, *, C, n_chunks, nb):
    ci = pl.program_id(1)
    C2 = 2 * C
    rows = nb * C
    seqs = range(nb)
    rsl = [slice(C * s, C * (s + 1)) for s in seqs]

    @pl.when(ci == 0)
    def _():
        prev_scr[...] = ps_ref[...]
        z = jnp.zeros((HEAD_DIM, HEAD_DIM), F32)
        for s in seqs:
            for p in range(N_PAIRS):
                top = jnp.concatenate([s0_ref[s, 2 * p], z], axis=1)
                bot = jnp.concatenate([z, s0_ref[s, 2 * p + 1]], axis=1)
                s_scr[s * N_PAIRS + p] = jnp.concatenate([top, bot], axis=0)

    lane = lax.broadcasted_iota(jnp.int32, (1, LANES), 1)
    lane_lo = lane < HEAD_DIM
    ri = lax.broadcasted_iota(jnp.int32, (LANES, LANES), 0)
    cj = lax.broadcasted_iota(jnp.int32, (LANES, LANES), 1)
    seg_ones = jnp.where((ri < HEAD_DIM) == (cj < HEAD_DIM), 1.0, 0.0).astype(BF16)
    row_c = lax.broadcasted_iota(jnp.int32, (C, 1), 0)
    tri = jnp.where(lax.broadcasted_iota(jnp.int32, (C, C), 0) >= lax.broadcasted_iota(jnp.int32, (C, C), 1),
                    1.0, 0.0).astype(BF16)
    mid = C // 2 - 1

    def mixed_all(col0, width):
        cols = slice(col0, col0 + width)
        cur = p_ref[:, :, cols].astype(F32).reshape(rows, width)
        prev = jnp.concatenate([jnp.broadcast_to(prev_scr[s, :, cols], (C, width)) for s in seqs], axis=0)
        row_all = lax.broadcasted_iota(jnp.int32, (rows, 1), 0) % C
        shifted = jnp.where(row_all == 0, prev, pltpu.roll(cur, 1, 0))
        for s in seqs:
            prev_scr[s, :, cols] = cur[C * (s + 1) - 1:C * (s + 1), :]
        return cur + (shifted - cur) * mu_ref[:, cols]

    xl = mixed_all(3 * RWKV_W, LANES)
    gl = mixed_all(3 * RWKV_W + LANES, D_GATE)
    th, tl = _split(jnp.where(lane_lo, jnp.tanh(xl), xl))
    pre_scr[:, PRE_W:PRE_W + 2 * RWKV_W] = (_mm(th, lora_hi_ref[...])
                                            + (_mm(th, lora_lo_ref[...]) + _mm(tl, lora_hi_ref[...])))
    sgh, sgl = _split(_sigmoid_tanh(gl))
    pre_scr[:, PRE_GATE:PRE_GATE + RWKV_W] = (_mm(sgh, wg_hi_ref[...])
                                              + (_mm(sgh, wg_lo_ref[...]) + _mm(sgl, wg_hi_ref[...])))
    k_all = mixed_all(RWKV_W, RWKV_W)
    pre_scr[:, PRE_K:PRE_K + RWKV_W] = k_all
    kkr = k_all * kk_ref[...]
    for p in range(N_PAIRS):
        sl = slice(LANES * p, LANES * (p + 1))
        pre_scr[:, PRE_SQ + LANES * p:PRE_SQ + LANES * (p + 1)] = _mm_split_lhs((kkr * kkr)[:, sl], seg_ones)

    def stack(x):
        return jnp.concatenate([jnp.where(lane_lo, x, 0.0), jnp.where(lane_lo, 0.0, x)], axis=0)

    def unit_inputs(s, p):
        def mixed(col0):
            cols = slice(col0, col0 + LANES)
            cur = p_ref[s, :, cols].astype(F32)
            shifted = jnp.where(row_c == 0, prev_scr[s, :, cols], pltpu.roll(cur, 1, 0))
            prev_scr[s, :, cols] = cur[C - 1:C, :]
            return cur + (shifted - cur) * mu_ref[:, cols]

        sl = slice(LANES * p, LANES * (p + 1))
        rs = rsl[s]
        pre = lambda base: pre_scr[rs, base + LANES * p:base + LANES * (p + 1)]
        r = mixed(LANES * p)
        v = mixed(2 * RWKV_W + LANES * p)
        k = pre(PRE_K)
        logd = -EXP_M_HALF * _sigmoid(w0_ref[:, sl] + pre(PRE_W))
        a_lr = _sigmoid_tanh(a0_ref[:, sl] + pre(PRE_W + RWKV_W))
        kkn = k * kk_ref[:, sl] / jnp.maximum(jnp.sqrt(pre(PRE_SQ)), 1e-12)
        kmod = k * (1.0 + (a_lr - 1.0) * ka_ref[:, sl])
        c = _mm_split_lhs_rhs(tri, logd)
        c_mid = c[mid:mid + 1, :]
        c_end = c[C - 1:C, :]
        cp = c - c_mid
        e_pos = jnp.exp(cp)
        e_neg = jnp.exp(-cp)
        e_prev = jnp.where(row_c == 0, jnp.exp(-c_mid), pltpu.roll(e_pos, 1, 0))
        bonus_scr[rs, sl] = _mm1(r * kmod * rk_ref[:, sl], seg_ones) * v
        return dict(ar=jnp.concatenate([stack(-kkn * e_prev), stack(r * e_pos)], axis=0),
                    bk=jnp.concatenate([stack(kkn * a_lr * e_neg), stack(kmod * e_neg)], axis=0),
                    v=stack(v), e_mid=jnp.exp(c_mid), e_end=jnp.exp(c_end), e_endp=jnp.exp(c_end - c_mid))

    ri2 = lax.broadcasted_iota(jnp.int32, (C2, C2), 0)
    cj2 = lax.broadcasted_iota(jnp.int32, (C2, C2), 1)
    strict = cj2 < ri2
    incl = cj2 <= ri2
    n_sq = C.bit_length() - 2
    fused = C2 % LANES == 0
    pairs = range(nb * N_PAIRS)
    sls = [slice(LANES * p, LANES * (p + 1)) for _ in seqs for p in range(N_PAIRS)]
    units = [unit_inputs(s, p) for s in seqs for p in range(N_PAIRS)]
    ar = [un["ar"] for un in units]
    bk = [un["bk"] for un in units]
    v_st = [un["v"] for un in units]
    e_mid = [un["e_mid"] for un in units]
    e_end = [un["e_end"] for un in units]
    e_endp = [un["e_endp"] for un in units]
    if fused:
        tri_mask = jnp.concatenate([jnp.concatenate([strict, strict], axis=1),
                                    jnp.concatenate([incl, incl], axis=1)], axis=0)
        gram = [jnp.where(tri_mask, _mm_gram(ar[p], bk[p], NT), 0.0) for p in pairs]
        g_ab = [g[:C2, :C2] for g in gram]
        g_ak = [g[:C2, C2:] for g in gram]
        g_rb = [g[C2:, :C2] for g in gram]
        g_rk = [g[C2:, C2:] for g in gram]
    else:
        g_ab = [jnp.where(strict, _mm_gram(ar[p][:C2], bk[p][:C2], NT), 0.0) for p in pairs]
        g_ak = [jnp.where(strict, _mm_gram(ar[p][:C2], bk[p][C2:], NT), 0.0) for p in pairs]
        g_rb = [jnp.where(incl, _mm_gram(ar[p][C2:], bk[p][:C2], NT), 0.0) for p in pairs]
        g_rk = [jnp.where(incl, _mm_gram(ar[p][C2:], bk[p][C2:], NT), 0.0) for p in pairs]

    z2 = [_mm_apply(g_ak[p], v_st[p]) for p in pairs]
    x = [jnp.concatenate([ar[p][:C2] * e_mid[p], z2[p]], axis=1) for p in pairs]
    lp = g_ab
    x = [x[p] + _mm_apply(lp[p], x[p]) for p in pairs]
    for _ in range(n_sq):
        lp = [_mm_inv(lp[p], lp[p]) for p in pairs]
        x = [x[p] + _mm_apply(lp[p], x[p]) for p in pairs]

    s_old = [s_scr[p] for p in pairs]
    wr = [jnp.concatenate([x[p][:, :LANES], ar[p][C2:] * e_mid[p]], axis=0) for p in pairs]
    ws = [_mm_state(wr[p], s_old[p], NT) for p in pairs]
    u = [ws[p][:C2] + x[p][:, LANES:] for p in pairs]
    uv = [jnp.concatenate([u[p], v_st[p]], axis=0) for p in pairs]
    if fused:
        y_st = [ws[p][C2:] + _mm_state(jnp.concatenate([g_rb[p], g_rk[p]], axis=1), uv[p]) for p in pairs]
    else:
        y_st = [ws[p][C2:] + _mm_state(g_rb[p], u[p]) + _mm_state(g_rk[p], v_st[p]) for p in pairs]
    for p in pairs:
        s_scr[p] = s_old[p] * e_end[p] + _mm_state(uv[p], bk[p] * e_endp[p], TN)

    y = [jnp.concatenate([y_st[s * N_PAIRS + p][:C] + y_st[s * N_PAIRS + p][C:] for s in seqs], axis=0)
         for p in range(N_PAIRS)]
    for p in range(N_PAIRS):
        sl = sls[p]
        dev = y[p] - _mm1(y[p], seg_ones) * (1.0 / HEAD_DIM)
        var = _mm1(dev * dev, seg_ones) * (1.0 / HEAD_DIM)
        yn = dev * lax.rsqrt(var + LNX_EPS) * lnw_ref[:, sl] + lnb_ref[:, sl]
        gate = pre_scr[:, PRE_GATE + LANES * p:PRE_GATE + LANES * (p + 1)]
        y_ref[:, :, sl] = ((yn + bonus_scr[:, sl]) * gate).reshape(nb, C, LANES)

    @pl.when(ci == n_chunks - 1)
    def _():
        for s in seqs:
            for p in range(N_PAIRS):
                s_fin = s_scr[s * N_PAIRS + p]
                sfin_ref[s, 2 * p] = s_fin[:HEAD_DIM, :HEAD_DIM]
                sfin_ref[s, 2 * p + 1] = s_fin[HEAD_DIM:, HEAD_DIM:]


RWKV_SEQS = 8
PRE_W = 0
PRE_GATE = 2 * RWKV_W
PRE_K = 3 * RWKV_W
PRE_SQ = 4 * RWKV_W
PRE_COLS = 5 * RWKV_W


def _rwkv(proj_rw, pstart, s0, wts, C):
    b, t, _ = proj_rw.shape
    nb = RWKV_SEQS
    assert b % nb == 0 and t % C == 0
    n_chunks = t // C
    vec = lambda n: _const_spec((1, n))
    kern = functools.partial(_rwkv_kernel, C=C, n_chunks=n_chunks, nb=nb)
    return pl.pallas_call(
        kern,
        grid=(b // nb, n_chunks),
        in_specs=[pl.BlockSpec((nb, C, RWKV_COLS), lambda i, j: (i, j, 0)),
                  pl.BlockSpec((nb, 1, RWKV_COLS), lambda i, j: (i, 0, 0)),
                  pl.BlockSpec((nb, N_HEADS, HEAD_DIM, HEAD_DIM), lambda i, j: (i, 0, 0, 0)),
                  vec(RWKV_COLS), vec(RWKV_W), vec(RWKV_W),
                  _const_spec((LANES, 2 * RWKV_W)), _const_spec((LANES, 2 * RWKV_W)),
                  _const_spec((D_GATE, RWKV_W)), _const_spec((D_GATE, RWKV_W)),
                  vec(RWKV_W), vec(RWKV_W), vec(RWKV_W), vec(RWKV_W), vec(RWKV_W)],
        out_specs=[pl.BlockSpec((nb, C, RWKV_W), lambda i, j: (i, j, 0)),
                   pl.BlockSpec((nb, N_HEADS, HEAD_DIM, HEAD_DIM), lambda i, j: (i, 0, 0, 0))],
        out_shape=[jax.ShapeDtypeStruct((b, t, RWKV_W), F32),
                   jax.ShapeDtypeStruct((b, N_HEADS, HEAD_DIM, HEAD_DIM), F32)],
        scratch_shapes=[pltpu.VMEM((nb * N_PAIRS, LANES, LANES), F32), pltpu.VMEM((nb, 1, RWKV_COLS), F32),
                        pltpu.VMEM((nb * C, PRE_COLS), F32), pltpu.VMEM((nb * C, RWKV_W), F32)],
        compiler_params=_params(("parallel", "arbitrary")),
        name="rwkv",
    )(proj_rw, pstart, s0, *wts)


ATT_UNROLL = 4
LOG2E = 1.4426950408889634


def _attn_prompt_kernel(q_ref, k_ref, v_ref, bias_ref, o_ref, acc_scr, m_scr, l_scr, *, T):
    QB = N_STEPS
    lane = lax.broadcasted_iota(jnp.int32, (1, LANES), 1)
    lane_lo = lane < HEAD_DIM

    def halves(x):
        return jnp.where(lane_lo, x[:QB], x[QB:])

    order = sorted(DILATIONS, reverse=True)
    for bi, d in enumerate(order):
        per_res = T // (QB * d)
        ub = min(ATT_UNROLL, per_res)
        groups = ATT_UNROLL // ub
        gpr = per_res // ub

        def rows(start, d=d):
            if d == 1:
                return pl.ds(pl.multiple_of(start, QB), QB)
            return pl.ds(start, QB, stride=d)

        def body(it, carry, bi=bi, d=d, ub=ub, groups=groups, gpr=gpr, rows=rows):
            blocks = []
            for g in range(groups):
                gi = it * groups + g
                res = gi // gpr
                n0 = (gi % gpr) * ub
                starts = [res + d * QB * jnp.maximum(n0 - 1, 0)] + [res + d * QB * (n0 + u) for u in range(ub)]
                kb = [k_ref[rows(s), :].astype(BF16) for s in starts]
                vb = [v_ref[rows(s), :].astype(BF16) for s in starts]
                for u in range(ub):
                    bias = bias_ref[jnp.where(n0 == 0, 1, 0)] if u == 0 else bias_ref[0]
                    blocks.append((starts[u + 1], jnp.concatenate([kb[u], kb[u + 1]], axis=0),
                                   jnp.concatenate([vb[u], vb[u + 1]], axis=0), bias))
            q_st = []
            for start, _, _, _ in blocks:
                q = q_ref[rows(start), :] * (HEAD_DIM ** -0.5 * LOG2E)
                q_st.append(jnp.concatenate([jnp.where(lane_lo, q, 0.0), jnp.where(lane_lo, 0.0, q)],
                                            axis=0).astype(BF16))
            s = [_mm(q_st[i], blk[1], NT) + blk[3] for i, blk in enumerate(blocks)]
            m = [jnp.max(x, axis=1, keepdims=True) for x in s]
            p = [jnp.exp2(s[i] - m[i]) for i in range(len(blocks))]
            l = [jnp.sum(x, axis=1, keepdims=True) for x in p]
            o = [halves(_mm(p[i].astype(BF16), blk[2])) for i, blk in enumerate(blocks)]
            m_b = [halves(jnp.broadcast_to(x, (2 * QB, LANES))) for x in m]
            l_b = [halves(jnp.broadcast_to(x, (2 * QB, LANES))) for x in l]
            if bi == 0:
                for i, blk in enumerate(blocks):
                    acc_scr[rows(blk[0]), :] = o[i]
                    m_scr[rows(blk[0]), :] = m_b[i]
                    l_scr[rows(blk[0]), :] = l_b[i]
            else:
                m_old = [m_scr[rows(blk[0]), :] for blk in blocks]
                acc_old = [acc_scr[rows(blk[0]), :] for blk in blocks]
                l_old = [l_scr[rows(blk[0]), :] for blk in blocks]
                for i, blk in enumerate(blocks):
                    m_new = jnp.maximum(m_old[i], m_b[i])
                    w_old = jnp.exp2(m_old[i] - m_new)
                    w_new = jnp.exp2(m_b[i] - m_new)
                    acc = acc_old[i] * w_old + o[i] * w_new
                    den = l_old[i] * w_old + l_b[i] * w_new
                    if bi < len(order) - 1:
                        acc_scr[rows(blk[0]), :] = acc
                        m_scr[rows(blk[0]), :] = m_new
                        l_scr[rows(blk[0]), :] = den
                    else:
                        o_ref[rows(blk[0]), :] = acc / den
            return carry

        lax.fori_loop(0, d * gpr // groups, body, 0)


def _band_bias():
    qi = jnp.arange(2 * N_STEPS)[:, None] % N_STEPS
    kj = jnp.arange(2 * N_STEPS)[None, :]
    band = (kj >= qi) & (kj <= qi + N_STEPS)
    normal = jnp.where(band, 0.0, NEG_INF).astype(F32)
    first = jnp.where(band & (kj >= N_STEPS), 0.0, NEG_INF).astype(F32)
    return jnp.stack([normal, first])


def _attn_prompt(q, k, v):
    b, t, _ = q.shape
    assert t % (N_STEPS * max(DILATIONS)) == 0
    spec = pl.BlockSpec((None, t, LANES), lambda i, p: (i, 0, p))
    return pl.pallas_call(
        functools.partial(_attn_prompt_kernel, T=t),
        grid=(b, N_PAIRS),
        in_specs=[spec, spec, spec, _const_spec((2, 2 * N_STEPS, 2 * N_STEPS))],
        out_specs=spec,
        out_shape=jax.ShapeDtypeStruct((b, t, ATT_W), F32),
        scratch_shapes=[pltpu.VMEM((t, LANES), F32)] * 3,
        compiler_params=_params(("parallel", "parallel")),
        name="attn_prompt",
    )(q, k, v, _band_bias())


def _attn_sample_kernel(q_ref, kn_ref, vn_ref, kc_ref, vc_ref, cntc_ref, cntn_ref,
                        o_ref, ko_ref, vo_ref, *, ts, w_buf):
    rows = N_HEADS * ts
    ri = lax.broadcasted_iota(jnp.int32, (rows, ATT_W), 0)
    cj = lax.broadcasted_iota(jnp.int32, (rows, ATT_W), 1)
    own = (ri // ts) == (cj // HEAD_DIM)
    q_all = jnp.concatenate([q_ref[...] * (HEAD_DIM ** -0.5)] * N_HEADS, axis=0)
    q_st = jnp.where(own, q_all, 0.0).astype(BF16)
    kc = kc_ref[...]
    vc = vc_ref[...]
    kn = kn_ref[...]
    vn = vn_ref[...]
    cnt_c = cntc_ref[...]
    cnt_n = cntn_ref[...]
    s_c = jnp.where(cnt_c > 0.0, _mm(q_st, kc.astype(BF16), NT), NEG_INF)
    s_n = jnp.where(cnt_n > 0.0, _mm(q_st, kn.astype(BF16), NT), NEG_INF)
    m = jnp.maximum(jnp.max(s_c, axis=1, keepdims=True), jnp.max(s_n, axis=1, keepdims=True))
    p_c = cnt_c * jnp.exp(s_c - m)
    p_n = cnt_n * jnp.exp(s_n - m)
    den = jnp.sum(p_c, axis=1, keepdims=True) + jnp.sum(p_n, axis=1, keepdims=True)
    o_st = (_mm(p_c.astype(BF16), vc.astype(BF16)) + _mm(p_n.astype(BF16), vn.astype(BF16))) / den
    o_st = jnp.where(own, o_st, 0.0)
    out = o_st[0:ts]
    for h in range(1, N_HEADS):
        out = out + o_st[h * ts:(h + 1) * ts]
    o_ref[...] = out
    ko_ref[0:w_buf - ts, :] = kc[ts:w_buf]
    ko_ref[w_buf - ts:w_buf, :] = kn
    vo_ref[0:w_buf - ts, :] = vc[ts:w_buf]
    vo_ref[w_buf - ts:w_buf, :] = vn


def _branch_counts(ts, w_buf):
    delta = (w_buf + jnp.arange(ts)[:, None]) - jnp.arange(w_buf + ts)[None, :]
    cnt = jnp.zeros(delta.shape, F32)
    for d in DILATIONS:
        cnt = cnt + ((delta >= 0) & (delta % d == 0) & (delta // d <= N_STEPS)).astype(F32)
    cnt = jnp.tile(cnt, (N_HEADS, 1))
    return cnt[:, :w_buf], cnt[:, w_buf:]


def _attn_sample(q, kn, vn, kcache, vcache):
    b, ts, _ = q.shape
    w_buf = kcache.shape[1]
    assert w_buf >= N_STEPS * max(DILATIONS) and ts % SUBLANES == 0
    cnt_c, cnt_n = _branch_counts(ts, w_buf)
    new = pl.BlockSpec((None, ts, ATT_W), lambda i: (i, 0, 0))
    cache = pl.BlockSpec((None, w_buf, ATT_W), lambda i: (i, 0, 0))
    return pl.pallas_call(
        functools.partial(_attn_sample_kernel, ts=ts, w_buf=w_buf),
        grid=(b,),
        in_specs=[new, new, new, cache, cache,
                  _const_spec((N_HEADS * ts, w_buf)), _const_spec((N_HEADS * ts, ts))],
        out_specs=[new, cache, cache],
        out_shape=[jax.ShapeDtypeStruct((b, ts, ATT_W), F32),
                   jax.ShapeDtypeStruct((b, w_buf, ATT_W), F32),
                   jax.ShapeDtypeStruct((b, w_buf, ATT_W), F32)],
        compiler_params=_params(("parallel",)),
        name="attn_sample",
    )(q, kn, vn, kcache, vcache, cnt_c, cnt_n)


FFN_TF = 256
N_FCHUNK = D_FF // FFN_TF


def _mlp_kernel(*refs, tm, tiles_per_seq, seq_len, has_state):
    if has_state:
        (x_ref, rw_ref, att_ref, wout_ref, gmix_ref, gpre_ref, gpost_ref, wup_ref, cw_ref, cb_ref,
         wd_ref, p2_ref, y_ref, u_ref, act_scr) = refs
    else:
        (x_ref, rw_ref, att_ref, wout_ref, gmix_ref, gpre_ref, gpost_ref, wup_ref, cw_ref, cb_ref,
         wd_ref, y_ref, u_ref, act_scr, ubuf_scr, carry_scr) = refs

        @pl.when((pl.program_id(0) % tiles_per_seq) == 0)
        def _():
            carry_scr[...] = jnp.zeros_like(carry_scr)

    mix = (_mm(rw_ref[...].astype(BF16), wout_ref[:RWKV_W, :])
           + _mm(att_ref[...].astype(BF16), wout_ref[RWKV_W:, :]))
    x1 = x_ref[...] + _rms(mix, gmix_ref[...])
    hb = _rms(x1, gpre_ref[...]).astype(BF16)
    t_in = lax.broadcasted_iota(jnp.int32, (tm, 1), 0) % seq_len

    for c in range(N_FCHUNK):
        conv = []
        for idx, col0 in enumerate((c * FFN_TF, D_FF + c * FFN_TF)):
            cols = slice(col0, col0 + FFN_TF)
            u = _mm(hb, wup_ref[:, cols])
            if has_state:
                p2 = p2_ref[:, cols]
                u1 = jnp.where(t_in >= 1, pltpu.roll(u, 1, 0), pltpu.roll(p2, tm - 1, 0))
                u2 = jnp.where(t_in >= 2, pltpu.roll(u, 2, 0), p2)
                u_ref[:, cols] = u
            else:
                stage = ubuf_scr.at[c % 2, idx]
                stage[0:SUBLANES, :] = carry_scr[c, idx]
                stage[SUBLANES:, :] = u
                u1 = stage[SUBLANES - 1:SUBLANES - 1 + tm, :]
                u2 = stage[SUBLANES - 2:SUBLANES - 2 + tm, :]
                carry_scr[c, idx] = u[tm - SUBLANES:]
                u_ref[:, cols] = u[tm - SUBLANES:]
            conv.append(u2 * cw_ref[0:1, cols] + u1 * cw_ref[1:2, cols] + u * cw_ref[2:3, cols]
                        + cb_ref[:, cols])
        act_scr[:, c * FFN_TF:(c + 1) * FFN_TF] = (conv[0] * _sigmoid(conv[0]) * conv[1]).astype(BF16)

    y_ref[...] = x1 + _rms(_mm(act_scr[...], wd_ref[...]), gpost_ref[...])


def _mlp(x2d, rw, att, w, tm, seq_len, state_rows=None):
    n = x2d.shape[0]
    has_state = state_rows is not None
    tiles_per_seq = max(seq_len // tm, 1)
    n_tiles = n // tm
    row = lambda i: (i, 0)
    vec = lambda: _const_spec((1, D_MODEL))
    in_specs = [pl.BlockSpec((tm, D_MODEL), row), pl.BlockSpec((tm, RWKV_W), row),
                pl.BlockSpec((tm, ATT_W), row), _const_spec((D_MODEL, D_MODEL)), vec(), vec(), vec(),
                _const_spec((D_MODEL, 2 * D_FF)), _const_spec((CONV_W, 2 * D_FF)),
                _const_spec((1, 2 * D_FF)), _const_spec((D_FF, D_MODEL))]
    args = [x2d, rw, att, w["w_out"], w["norm_mix_post"], w["norm_ffn_pre"], w["norm_ffn_post"],
            w["w_ffn_up"], w["ffn_conv_w"], w["ffn_conv_b"], w["w_ffn_down"]]
    scratch = [pltpu.VMEM((tm, D_FF), BF16)]
    if has_state:
        assert tm % seq_len == 0 and seq_len == SUBLANES
        in_specs.append(pl.BlockSpec((tm, 2 * D_FF), row))
        args.append(state_rows)
        u_spec = pl.BlockSpec((tm, 2 * D_FF), row)
        u_shape = jax.ShapeDtypeStruct((n, 2 * D_FF), F32)
    else:
        assert seq_len % tm == 0
        scratch += [pltpu.VMEM((2, 2, tm + SUBLANES, FFN_TF), F32),
                    pltpu.VMEM((N_FCHUNK, 2, SUBLANES, FFN_TF), F32)]
        u_spec = pl.BlockSpec((None, SUBLANES, 2 * D_FF), lambda i: (i, 0, 0))
        u_shape = jax.ShapeDtypeStruct((n_tiles, SUBLANES, 2 * D_FF), F32)
    kern = functools.partial(_mlp_kernel, tm=tm, tiles_per_seq=tiles_per_seq, seq_len=seq_len,
                             has_state=has_state)
    return pl.pallas_call(
        kern,
        grid=(n_tiles,),
        in_specs=in_specs,
        out_specs=[pl.BlockSpec((tm, D_MODEL), row), u_spec],
        out_shape=[jax.ShapeDtypeStruct((n, D_MODEL), F32), u_shape],
        scratch_shapes=scratch,
        compiler_params=_params(("arbitrary",)),
        name="mlp",
    )(*args)


def _rope_tables(pos):
    half = HEAD_DIM // 2
    inv = ROPE_THETA ** (-jnp.arange(half, dtype=F32) / half)
    ang = pos.astype(F32)[:, None] * inv[None, :]
    cos, sin = jnp.cos(ang), jnp.sin(ang)
    cos_t = jnp.tile(jnp.concatenate([cos, cos], axis=1), (1, LANES // HEAD_DIM))
    sin_t = jnp.tile(jnp.concatenate([-sin, sin], axis=1), (1, LANES // HEAD_DIM))
    return cos_t, sin_t


def _layer(x, pos, h_prev, wkv0, k_buf, v_buf, conv_prev, w, tm, chunk, mlp_tm):
    b, t, _ = x.shape
    n = b * t
    x2d = x.reshape(n, D_MODEL)
    cos_t, sin_t = _rope_tables(pos)
    if t < tm:
        cos_t = jnp.tile(cos_t, (tm // t, 1))
        sin_t = jnp.tile(sin_t, (tm // t, 1))
        pos_tiles = 1
    else:
        pos_tiles = t // tm
    keep = min(N_STEPS * max(DILATIONS), t)
    tail_tiles = keep // tm if k_buf is None else 0
    rw, q, k, v, *tails = _in_proj(x2d, w["norm_mix_pre"], w["w_in"], cos_t, sin_t, tm, pos_tiles, tail_tiles)
    h_last, pstart = _shift_state(x[:, -1], w["norm_mix_pre"], h_prev, w["w_in_rw"])

    rwkv_out, wkv_new = _rwkv(rw.reshape(b, t, RWKV_COLS), pstart.reshape(b, 1, RWKV_COLS), wkv0,
                              w["rwkv"], chunk)
    q3, k3, v3 = (a.reshape(b, t, ATT_W) for a in (q, k, v))
    if k_buf is None:
        att = _attn_prompt(q3, k3, v3)
        k_new, v_new = tails
    else:
        w_buf = k_buf.shape[1]
        att, k_new, v_new = _attn_sample(q3, k3, v3, k_buf.reshape(b, w_buf, ATT_W),
                                         v_buf.reshape(b, w_buf, ATT_W))
    rw2d, att2d = rwkv_out.reshape(n, RWKV_W), att.reshape(n, ATT_W)
    if conv_prev is None:
        y, tails = _mlp(x2d, rw2d, att2d, w, mlp_tm, t)
        conv_new = tails.reshape(b, t // mlp_tm, SUBLANES, 2 * D_FF)[:, -1, SUBLANES - (CONV_W - 1):]
    else:
        p2 = jnp.pad(conv_prev, ((0, 0), (0, t - (CONV_W - 1)), (0, 0))).reshape(n, 2 * D_FF)
        y, u = _mlp(x2d, rw2d, att2d, w, mlp_tm, t, state_rows=p2)
        conv_new = u.reshape(b, t, 2 * D_FF)[:, t - (CONV_W - 1):]
    heads = lambda a: a.reshape(b, a.shape[1], N_HEADS, HEAD_DIM)
    return y.reshape(b, t, D_MODEL), h_last, wkv_new, heads(k_new), heads(v_new), conv_new


def _prep_weights(norm_mix_pre, norm_mix_post, norm_ffn_pre, norm_ffn_post, w_in, mu_shift, w0,
                  w_decay_up, a0, w_iclr_up, w_gate_up, k_k, k_a, r_k, lnx_w, lnx_b, w_out,
                  w_ffn_up, ffn_conv_w, ffn_conv_b, w_ffn_down):
    vec = lambda a: a.reshape(1, -1)
    zero = jnp.zeros((D_DECAY, RWKV_W), F32)
    lora = jnp.concatenate([jnp.concatenate([w_decay_up, zero], axis=1),
                            jnp.concatenate([zero, w_iclr_up], axis=1)], axis=0)
    lora_hi, lora_lo = _split(lora)
    wg_hi, wg_lo = _split(w_gate_up)
    w_in_bf = w_in.astype(BF16)
    return {
        "norm_mix_pre": vec(norm_mix_pre), "norm_mix_post": vec(norm_mix_post),
        "norm_ffn_pre": vec(norm_ffn_pre), "norm_ffn_post": vec(norm_ffn_post),
        "w_in": w_in_bf, "w_in_rw": w_in_bf[:, :RWKV_COLS],
        "rwkv": (vec(mu_shift), vec(w0), vec(a0), lora_hi, lora_lo, wg_hi, wg_lo,
                 vec(k_k), vec(k_a), vec(r_k), vec(lnx_w), vec(lnx_b)),
        "w_out": w_out.astype(BF16), "w_ffn_up": w_ffn_up.astype(BF16),
        "ffn_conv_w": ffn_conv_w, "ffn_conv_b": vec(ffn_conv_b), "w_ffn_down": w_ffn_down.astype(BF16),
    }


PROMPT_TM = 512
PROMPT_CHUNK = 64
SAMPLE_MLP_TM = 128


def kernel(x_prompt, x_sample, state_rwkv_shift, state_rwkv_wkv, cache_att_k, cache_att_v, state_ffn_conv, norm_mix_pre, norm_mix_post, norm_ffn_pre, norm_ffn_post, w_in, mu_shift, w0, w_decay_up, a0, w_iclr_up, w_gate_up, k_k, k_a, r_k, lnx_w, lnx_b, w_out, w_ffn_up, ffn_conv_w, ffn_conv_b, w_ffn_down):
    bp, tp, _ = x_prompt.shape
    bs, ts, _ = x_sample.shape
    depth = norm_mix_pre.shape[0]
    pos_p = jnp.arange(tp, dtype=jnp.int32)
    pos_s = PAST_LEN + jnp.arange(ts, dtype=jnp.int32)
    yp, ys = x_prompt, x_sample
    outs_p = [[] for _ in range(5)]
    outs_s = [[] for _ in range(5)]
    for l in range(depth):
        w = _prep_weights(norm_mix_pre[l], norm_mix_post[l], norm_ffn_pre[l], norm_ffn_post[l], w_in[l],
                          mu_shift[l], w0[l], w_decay_up[l], a0[l], w_iclr_up[l], w_gate_up[l], k_k[l],
                          k_a[l], r_k[l], lnx_w[l], lnx_b[l], w_out[l], w_ffn_up[l], ffn_conv_w[l],
                          ffn_conv_b[l], w_ffn_down[l])
        yp, *state_p = _layer(yp, pos_p, jnp.zeros((bp, D_MODEL), F32),
                              jnp.zeros((bp, N_HEADS, HEAD_DIM, HEAD_DIM), F32), None, None, None,
                              w, min(PROMPT_TM, tp), min(PROMPT_CHUNK, tp), min(PROMPT_TM, tp))
        ys, *state_s = _layer(ys, pos_s, state_rwkv_shift[l], state_rwkv_wkv[l], cache_att_k[l],
                              cache_att_v[l], state_ffn_conv[l], w, bs * ts, ts, SAMPLE_MLP_TM)
        for acc, val in zip(outs_p, state_p):
            acc.append(val)
        for acc, val in zip(outs_s, state_s):
            acc.append(val)
    return (yp, ys, *(jnp.stack(a) for a in outs_p), *(jnp.stack(a) for a in outs_s))
```

```python
import functools

import jax
import jax.numpy as jnp
from jax import lax
from jax.experimental import pallas as pl
from jax.experimental.pallas import tpu as pltpu

F32 = jnp.float32
BF16 = jnp.bfloat16

D_MODEL = 1024
HEAD_DIM = 64
RWKV_W = 512
ATT_W = 512
N_HEADS = 8
N_PAIRS = N_HEADS // 2
D_DECAY = 64
D_ICLR = 64
D_GATE = 128
RWKV_COLS = 3 * RWKV_W + D_DECAY + D_ICLR + D_GATE
D_IN = RWKV_COLS + 3 * ATT_W
DILATIONS = (1, 4, 16)
N_STEPS = 128
ROPE_THETA = 10000.0
D_FF = 2816
CONV_W = 3
NORM_EPS = 1e-6
LNX_EPS = 64e-5
NEG_INF = -1e30
PAST_LEN = 16384

LANES = 128
SUBLANES = 8
VMEM_LIMIT = 56 * 1024 * 1024

NN = (((1,), (0,)), ((), ()))
NT = (((1,), (1,)), ((), ()))
TN = (((0,), (0,)), ((), ()))


def _mm(a, b, dims=NN):
    return lax.dot_general(a, b, dims, preferred_element_type=F32)


def _split(x):
    hi = x.astype(BF16)
    lo = (x - hi.astype(F32)).astype(BF16)
    return hi, lo


def _mm1(a, b, dims=NN):
    return _mm(a.astype(BF16), b.astype(BF16), dims)


_mm_gram = _mm1
_mm_inv = _mm1
_mm_apply = _mm1
_mm_state = _mm1


def _mm_split_lhs(a, b_bf16):
    ah, al = _split(a)
    return _mm(ah, b_bf16) + _mm(al, b_bf16)


def _mm_split_lhs_rhs(a_bf16, b):
    bh, bl = _split(b)
    return _mm(a_bf16, bh) + _mm(a_bf16, bl)


def _rms(x, g):
    return x * lax.rsqrt(jnp.mean(x * x, axis=-1, keepdims=True) + NORM_EPS) * g


def _sigmoid(x):
    return 1.0 / (1.0 + jnp.exp(-x))


def _sigmoid_tanh(x):
    return 0.5 + 0.5 * jnp.tanh(0.5 * x)


EXP_M_HALF = 0.6065306597126334


def _params(sem):
    return pltpu.CompilerParams(dimension_semantics=sem, vmem_limit_bytes=VMEM_LIMIT)


def _const_spec(shape):
    nd = len(shape)
    return pl.BlockSpec(shape, lambda *_: (0,) * nd, pipeline_mode=pl.Buffered(1))


def _in_proj_kernel(x_ref, g_ref, w_ref, cos_ref, sin_ref, rw_ref, qkv_ref, *tail_refs, pos_tiles):
    lane = lax.broadcasted_iota(jnp.int32, (1, ATT_W), 1)
    first_half = (lane % HEAD_DIM) < HEAD_DIM // 2
    q0 = RWKV_COLS
    tm = x_ref.shape[0]
    hb = _rms(x_ref[...], g_ref[...]).astype(BF16)
    rw_ref[...] = _mm(hb, w_ref[:, :RWKV_COLS]).astype(BF16)
    pos_rows = pl.ds(pl.multiple_of((pl.program_id(0) % pos_tiles) * tm, tm), tm)
    cos = jnp.concatenate([cos_ref[pos_rows, :]] * (ATT_W // LANES), axis=1)
    sin = jnp.concatenate([sin_ref[pos_rows, :]] * (ATT_W // LANES), axis=1)

    def rope(t):
        partner = jnp.where(first_half, pltpu.roll(t, ATT_W - HEAD_DIM // 2, 1),
                            pltpu.roll(t, HEAD_DIM // 2, 1))
        return t * cos + partner * sin

    qkv_ref[:, 0:ATT_W] = rope(_mm(hb, w_ref[:, q0:q0 + ATT_W]))
    k = rope(_mm(hb, w_ref[:, q0 + ATT_W:q0 + 2 * ATT_W]))
    v = _mm(hb, w_ref[:, q0 + 2 * ATT_W:q0 + 3 * ATT_W])
    qkv_ref[:, ATT_W:2 * ATT_W] = k
    qkv_ref[:, 2 * ATT_W:3 * ATT_W] = v
    if tail_refs:
        tail_refs[0][...] = k
        tail_refs[1][...] = v


def _in_proj(x2d, g, w_bf, cos_t, sin_t, tm, pos_tiles, tail_tiles=0):
    n = x2d.shape[0]
    row = lambda i: (i, 0)
    out_specs = [pl.BlockSpec((tm, RWKV_COLS), row), pl.BlockSpec((tm, 3 * ATT_W), row)]
    out_shape = [jax.ShapeDtypeStruct((n, RWKV_COLS), BF16), jax.ShapeDtypeStruct((n, 3 * ATT_W), F32)]
    if tail_tiles:
        skip = pos_tiles - tail_tiles
        tail = lambda i: (i // pos_tiles, jnp.maximum(i % pos_tiles - skip, 0), 0)
        out_specs += [pl.BlockSpec((None, tm, ATT_W), tail)] * 2
        out_shape += [jax.ShapeDtypeStruct((n // (pos_tiles * tm), tail_tiles * tm, ATT_W), F32)] * 2
    table = _const_spec((pos_tiles * tm, LANES))
    return pl.pallas_call(
        functools.partial(_in_proj_kernel, pos_tiles=pos_tiles),
        grid=(n // tm,),
        in_specs=[pl.BlockSpec((tm, D_MODEL), row), _const_spec((1, D_MODEL)),
                  _const_spec((D_MODEL, D_IN)), table, table],
        out_specs=out_specs,
        out_shape=out_shape,
        compiler_params=_params(("arbitrary",)),
        name="in_proj",
    )(x2d, g, w_bf, cos_t, sin_t)


def _shift_state_kernel(x_ref, g_ref, h_ref, w_ref, hl_ref, ps_ref):
    hl_ref[...] = _rms(x_ref[...], g_ref[...])
    ps_ref[...] = _mm(h_ref[...].astype(BF16), w_ref[...])


def _shift_state(x_last, g, h_prev, w_rw_bf):
    b = x_last.shape[0]
    return pl.pallas_call(
        _shift_state_kernel,
        out_shape=[jax.ShapeDtypeStruct((b, D_MODEL), F32), jax.ShapeDtypeStruct((b, RWKV_COLS), F32)],
        compiler_params=_params(None),
        name="shift_state",
    )(x_last, g, h_prev, w_rw_bf)


def _rwkv_kernel(p_ref, ps_ref, s0_ref, mu_ref, w0_ref, a0_ref, lora_hi_ref, lora_lo_ref,
                 wg_hi_ref, wg_lo_ref, kk_ref, ka_ref, rk_ref, lnw_ref, lnb_ref,
                 y_ref, sfin_ref, s_scr, prev_scr, pre_scr, bonus_scr, *, C, n_chunks, nb):
    ci = pl.program_id(1)
    C2 = 2 * C
    rows = nb * C
    seqs = range(nb)
    rsl = [slice(C * s, C * (s + 1)) for s in seqs]

    @pl.when(ci == 0)
    def _():
        prev_scr[...] = ps_ref[...]
        z = jnp.zeros((HEAD_DIM, HEAD_DIM), F32)
        for s in seqs:
            for p in range(N_PAIRS):
                top = jnp.concatenate([s0_ref[s, 2 * p], z], axis=1)
                bot = jnp.concatenate([z, s0_ref[s, 2 * p + 1]], axis=1)
                s_scr[s * N_PAIRS + p] = jnp.concatenate([top, bot], axis=0)

    lane = lax.broadcasted_iota(jnp.int32, (1, LANES), 1)
    lane_lo = lane < HEAD_DIM
    ri = lax.broadcasted_iota(jnp.int32, (LANES, LANES), 0)
    cj = lax.broadcasted_iota(jnp.int32, (LANES, LANES), 1)
    seg_ones = jnp.where((ri < HEAD_DIM) == (cj < HEAD_DIM), 1.0, 0.0).astype(BF16)
    row_c = lax.broadcasted_iota(jnp.int32, (C, 1), 0)
    tri = jnp.where(lax.broadcasted_iota(jnp.int32, (C, C), 0) >= lax.broadcasted_iota(jnp.int32, (C, C), 1),
                    1.0, 0.0).astype(BF16)
    mid = C // 2 - 1

    def mixed_all(col0, width):
        cols = slice(col0, col0 + width)
        cur = p_ref[:, :, cols].astype(F32).reshape(rows, width)
        prev = jnp.concatenate([jnp.broadcast_to(prev_scr[s, :, cols], (C, width)) for s in seqs], axis=0)
        row_all = lax.broadcasted_iota(jnp.int32, (rows, 1), 0) % C
        shifted = jnp.where(row_all == 0, prev, pltpu.roll(cur, 1, 0))
        for s in seqs:
            prev_scr[s, :, cols] = cur[C * (s + 1) - 1:C * (s + 1), :]
        return cur + (shifted - cur) * mu_ref[:, cols]

    xl = mixed_all(3 * RWKV_W, LANES)
    gl = mixed_all(3 * RWKV_W + LANES, D_GATE)
    th, tl = _split(jnp.where(lane_lo, jnp.tanh(xl), xl))
    pre_scr[:, PRE_W:PRE_W + 2 * RWKV_W] = (_mm(th, lora_hi_ref[...])
                                            + (_mm(th, lora_lo_ref[...]) + _mm(tl, lora_hi_ref[...])))
    sgh, sgl = _split(_sigmoid_tanh(gl))
    pre_scr[:, PRE_GATE:PRE_GATE + RWKV_W] = (_mm(sgh, wg_hi_ref[...])
                                              + (_mm(sgh, wg_lo_ref[...]) + _mm(sgl, wg_hi_ref[...])))
    k_all = mixed_all(RWKV_W, RWKV_W)
    pre_scr[:, PRE_K:PRE_K + RWKV_W] = k_all
    kkr = k_all * kk_ref[...]
    for p in range(N_PAIRS):
        sl = slice(LANES * p, LANES * (p + 1))
        pre_scr[:, PRE_SQ + LANES * p:PRE_SQ + LANES * (p + 1)] = _mm_split_lhs((kkr * kkr)[:, sl], seg_ones)

    def stack(x):
        return jnp.concatenate([jnp.where(lane_lo, x, 0.0), jnp.where(lane_lo, 0.0, x)], axis=0)

    def unit_inputs(s, p):
        def mixed(col0):
            cols = slice(col0, col0 + LANES)
            cur = p_ref[s, :, cols].astype(F32)
            shifted = jnp.where(row_c == 0, prev_scr[s, :, cols], pltpu.roll(cur, 1, 0))
            prev_scr[s, :, cols] = cur[C - 1:C, :]
            return cur + (shifted - cur) * mu_ref[:, cols]

        sl = slice(LANES * p, LANES * (p + 1))
        rs = rsl[s]
        pre = lambda base: pre_scr[rs, base + LANES * p:base + LANES * (p + 1)]
        r = mixed(LANES * p)
        v = mixed(2 * RWKV_W + LANES * p)
        k = pre(PRE_K)
        logd = -EXP_M_HALF * _sigmoid(w0_ref[:, sl] + pre(PRE_W))
        a_lr = _sigmoid_tanh(a0_ref[:, sl] + pre(PRE_W + RWKV_W))
        kkn = k * kk_ref[:, sl] / jnp.maximum(jnp.sqrt(pre(PRE_SQ)), 1e-12)
        kmod = k * (1.0 + (a_lr - 1.0) * ka_ref[:, sl])
        c = _mm_split_lhs_rhs(tri, logd)
        c_mid = c[mid:mid + 1, :]
        c_end = c[C - 1:C, :]
        cp = c - c_mid
        e_pos = jnp.exp(cp)
        e_neg = jnp.exp(-cp)
        e_prev = jnp.where(row_c == 0, jnp.exp(-c_mid), pltpu.roll(e_pos, 1, 0))
        bonus_scr[rs, sl] = _mm1(r * kmod * rk_ref[:, sl], seg_ones) * v
        return dict(ar=jnp.concatenate([stack(-kkn * e_prev), stack(r * e_pos)], axis=0),
                    bk=jnp.concatenate([stack(kkn * a_lr * e_neg), stack(kmod * e_neg)], axis=0),
                    v=stack(v), e_mid=jnp.exp(c_mid), e_end=jnp.exp(c_end), e_endp=jnp.exp(c_end - c_mid))

    ri2 = lax.broadcasted_iota(jnp.int32, (C2, C2), 0)
    cj2 = lax.broadcasted_iota(jnp.int32, (C2, C2), 1)
    strict = cj2 < ri2
    incl = cj2 <= ri2
    n_sq = C.bit_length() - 2
    fused = C2 % LANES == 0
    pairs = range(nb * N_PAIRS)
    sls = [slice(LANES * p, LANES * (p + 1)) for _ in seqs for p in range(N_PAIRS)]
    units = [unit_inputs(s, p) for s in seqs for p in range(N_PAIRS)]
    ar = [un["ar"] for un in units]
    bk = [un["bk"] for un in units]
    v_st = [un["v"] for un in units]
    e_mid = [un["e_mid"] for un in units]
    e_end = [un["e_end"] for un in units]
    e_endp = [un["e_endp"] for un in units]
    if fused:
        tri_mask = jnp.concatenate([jnp.concatenate([strict, strict], axis=1),
                                    jnp.concatenate([incl, incl], axis=1)], axis=0)
        gram = [jnp.where(tri_mask, _mm_gram(ar[p], bk[p], NT), 0.0) for p in pairs]
        g_ab = [g[:C2, :C2] for g in gram]
        g_ak = [g[:C2, C2:] for g in gram]
        g_rb = [g[C2:, :C2] for g in gram]
        g_rk = [g[C2:, C2:] for g in gram]
    else:
        g_ab = [jnp.where(strict, _mm_gram(ar[p][:C2], bk[p][:C2], NT), 0.0) for p in pairs]
        g_ak = [jnp.where(strict, _mm_gram(ar[p][:C2], bk[p][C2:], NT), 0.0) for p in pairs]
        g_rb = [jnp.where(incl, _mm_gram(ar[p][C2:], bk[p][:C2], NT), 0.0) for p in pairs]
        g_rk = [jnp.where(incl, _mm_gram(ar[p][C2:], bk[p][C2:], NT), 0.0) for p in pairs]

    z2 = [_mm_apply(g_ak[p], v_st[p]) for p in pairs]
    x = [jnp.concatenate([ar[p][:C2] * e_mid[p], z2[p]], axis=1) for p in pairs]
    lp = g_ab
    x = [x[p] + _mm_apply(lp[p], x[p]) for p in pairs]
    for _ in range(n_sq):
        lp = [_mm_inv(lp[p], lp[p]) for p in pairs]
        x = [x[p] + _mm_apply(lp[p], x[p]) for p in pairs]

    s_old = [s_scr[p] for p in pairs]
    wr = [jnp.concatenate([x[p][:, :LANES], ar[p][C2:] * e_mid[p]], axis=0) for p in pairs]
    ws = [_mm_state(wr[p], s_old[p], NT) for p in pairs]
    u = [ws[p][:C2] + x[p][:, LANES:] for p in pairs]
    uv = [jnp.concatenate([u[p], v_st[p]], axis=0) for p in pairs]
    if fused:
        y_st = [ws[p][C2:] + _mm_state(jnp.concatenate([g_rb[p], g_rk[p]], axis=1), uv[p]) for p in pairs]
    else:
        y_st = [ws[p][C2:] + _mm_state(g_rb[p], u[p]) + _mm_state(g_rk[p], v_st[p]) for p in pairs]
    for p in pairs:
        s_scr[p] = s_old[p] * e_end[p] + _mm_state(uv[p], bk[p] * e_endp[p], TN)

    y = [jnp.concatenate([y_st[s * N_PAIRS + p][:C] + y_st[s * N_PAIRS + p][C:] for s in seqs], axis=0)
         for p in range(N_PAIRS)]
    for p in range(N_PAIRS):
        sl = sls[p]
        dev = y[p] - _mm1(y[p], seg_ones) * (1.0 / HEAD_DIM)
        var = _mm1(dev * dev, seg_ones) * (1.0 / HEAD_DIM)
        yn = dev * lax.rsqrt(var + LNX_EPS) * lnw_ref[:, sl] + lnb_ref[:, sl]
        gate = pre_scr[:, PRE_GATE + LANES * p:PRE_GATE + LANES * (p + 1)]
        y_ref[:, :, sl] = ((yn + bonus_scr[:, sl]) * gate).reshape(nb, C, LANES)

    @pl.when(ci == n_chunks - 1)
    def _():
        for s in seqs:
            for p in range(N_PAIRS):
                s_fin = s_scr[s * N_PAIRS + p]
                sfin_ref[s, 2 * p] = s_fin[:HEAD_DIM, :HEAD_DIM]
                sfin_ref[s, 2 * p + 1] = s_fin[HEAD_DIM:, HEAD_DIM:]


RWKV_SEQS = 8
PRE_W = 0
PRE_GATE = 2 * RWKV_W
PRE_K = 3 * RWKV_W
PRE_SQ = 4 * RWKV_W
PRE_COLS = 5 * RWKV_W


def _rwkv(proj_rw, pstart, s0, wts, C):
    b, t, _ = proj_rw.shape
    nb = RWKV_SEQS
    assert b % nb == 0 and t % C == 0
    n_chunks = t // C
    vec = lambda n: _const_spec((1, n))
    kern = functools.partial(_rwkv_kernel, C=C, n_chunks=n_chunks, nb=nb)
    return pl.pallas_call(
        kern,
        grid=(b // nb, n_chunks),
        in_specs=[pl.BlockSpec((nb, C, RWKV_COLS), lambda i, j: (i, j, 0)),
                  pl.BlockSpec((nb, 1, RWKV_COLS), lambda i, j: (i, 0, 0)),
                  pl.BlockSpec((nb, N_HEADS, HEAD_DIM, HEAD_DIM), lambda i, j: (i, 0, 0, 0)),
                  vec(RWKV_COLS), vec(RWKV_W), vec(RWKV_W),
                  _const_spec((LANES, 2 * RWKV_W)), _const_spec((LANES, 2 * RWKV_W)),
                  _const_spec((D_GATE, RWKV_W)), _const_spec((D_GATE, RWKV_W)),
                  vec(RWKV_W), vec(RWKV_W), vec(RWKV_W), vec(RWKV_W), vec(RWKV_W)],
        out_specs=[pl.BlockSpec((nb, C, RWKV_W), lambda i, j: (i, j, 0)),
                   pl.BlockSpec((nb, N_HEADS, HEAD_DIM, HEAD_DIM), lambda i, j: (i, 0, 0, 0))],
        out_shape=[jax.ShapeDtypeStruct((b, t, RWKV_W), F32),
                   jax.ShapeDtypeStruct((b, N_HEADS, HEAD_DIM, HEAD_DIM), F32)],
        scratch_shapes=[pltpu.VMEM((nb * N_PAIRS, LANES, LANES), F32), pltpu.VMEM((nb, 1, RWKV_COLS), F32),
                        pltpu.VMEM((nb * C, PRE_COLS), F32), pltpu.VMEM((nb * C, RWKV_W), F32)],
        compiler_params=_params(("parallel", "arbitrary")),
        name="rwkv",
    )(proj_rw, pstart, s0, *wts)


ATT_UNROLL = 4
LOG2E = 1.4426950408889634


def _attn_prompt_kernel(q_ref, k_ref, v_ref, bias_ref, o_ref, acc_scr, m_scr, l_scr, *, T):
    QB = N_STEPS
    lane = lax.broadcasted_iota(jnp.int32, (1, LANES), 1)
    lane_lo = lane < HEAD_DIM

    def halves(x):
        return jnp.where(lane_lo, x[:QB], x[QB:])

    order = sorted(DILATIONS, reverse=True)
    for bi, d in enumerate(order):
        per_res = T // (QB * d)
        ub = min(ATT_UNROLL, per_res)
        groups = ATT_UNROLL // ub
        gpr = per_res // ub

        def rows(start, d=d):
            if d == 1:
                return pl.ds(pl.multiple_of(start, QB), QB)
            return pl.ds(start, QB, stride=d)

        def body(it, carry, bi=bi, d=d, ub=ub, groups=groups, gpr=gpr, rows=rows):
            blocks = []
            for g in range(groups):
                gi = it * groups + g
                res = gi // gpr
                n0 = (gi % gpr) * ub
                starts = [res + d * QB * jnp.maximum(n0 - 1, 0)] + [res + d * QB * (n0 + u) for u in range(ub)]
                kb = [k_ref[rows(s), :].astype(BF16) for s in starts]
                vb = [v_ref[rows(s), :].astype(BF16) for s in starts]
                for u in range(ub):
                    bias = bias_ref[jnp.where(n0 == 0, 1, 0)] if u == 0 else bias_ref[0]
                    blocks.append((starts[u + 1], jnp.concatenate([kb[u], kb[u + 1]], axis=0),
                                   jnp.concatenate([vb[u], vb[u + 1]], axis=0), bias))
            q_st = []
            for start, _, _, _ in blocks:
                q = q_ref[rows(start), :] * (HEAD_DIM ** -0.5 * LOG2E)
                q_st.append(jnp.concatenate([jnp.where(lane_lo, q, 0.0), jnp.where(lane_lo, 0.0, q)],
                                            axis=0).astype(BF16))
            s = [_mm(q_st[i], blk[1], NT) + blk[3] for i, blk in enumerate(blocks)]
            m = [jnp.max(x, axis=1, keepdims=True) for x in s]
            p = [jnp.exp2(s[i] - m[i]) for i in range(len(blocks))]
            l = [jnp.sum(x, axis=1, keepdims=True) for x in p]
            o = [halves(_mm(p[i].astype(BF16), blk[2])) for i, blk in enumerate(blocks)]
            m_b = [halves(jnp.broadcast_to(x, (2 * QB, LANES))) for x in m]
            l_b = [halves(jnp.broadcast_to(x, (2 * QB, LANES))) for x in l]
            if bi == 0:
                for i, blk in enumerate(blocks):
                    acc_scr[rows(blk[0]), :] = o[i]
                    m_scr[rows(blk[0]), :] = m_b[i]
                    l_scr[rows(blk[0]), :] = l_b[i]
            else:
                m_old = [m_scr[rows(blk[0]), :] for blk in blocks]
                acc_old = [acc_scr[rows(blk[0]), :] for blk in blocks]
                l_old = [l_scr[rows(blk[0]), :] for blk in blocks]
                for i, blk in enumerate(blocks):
                    m_new = jnp.maximum(m_old[i], m_b[i])
                    w_old = jnp.exp2(m_old[i] - m_new)
                    w_new = jnp.exp2(m_b[i] - m_new)
                    acc = acc_old[i] * w_old + o[i] * w_new
                    den = l_old[i] * w_old + l_b[i] * w_new
                    if bi < len(order) - 1:
                        acc_scr[rows(blk[0]), :] = acc
                        m_scr[rows(blk[0]), :] = m_new
                        l_scr[rows(blk[0]), :] = den
                    else:
                        o_ref[rows(blk[0]), :] = acc / den
            return carry

        lax.fori_loop(0, d * gpr // groups, body, 0)


def _band_bias():
    qi = jnp.arange(2 * N_STEPS)[:, None] % N_STEPS
    kj = jnp.arange(2 * N_STEPS)[None, :]
    band = (kj >= qi) & (kj <= qi + N_STEPS)
    normal = jnp.where(band, 0.0, NEG_INF).astype(F32)
    first = jnp.where(band & (kj >= N_STEPS), 0.0, NEG_INF).astype(F32)
    return jnp.stack([normal, first])


def _attn_prompt(qkv):
    b, t, _ = qkv.shape
    assert t % (N_STEPS * max(DILATIONS)) == 0
    spec = pl.BlockSpec((None, t, LANES), lambda i, p: (i, 0, p))
    slab = lambda j: pl.BlockSpec((None, t, LANES), lambda i, p, j=j: (i, 0, j * N_PAIRS + p))
    q, k, v = qkv, qkv, qkv
    return pl.pallas_call(
        functools.partial(_attn_prompt_kernel, T=t),
        grid=(b, N_PAIRS),
        in_specs=[slab(0), slab(1), slab(2), _const_spec((2, 2 * N_STEPS, 2 * N_STEPS))],
        out_specs=spec,
        out_shape=jax.ShapeDtypeStruct((b, t, ATT_W), F32),
        scratch_shapes=[pltpu.VMEM((t, LANES), F32)] * 3,
        compiler_params=_params(("parallel", "parallel")),
        name="attn_prompt",
    )(q, k, v, _band_bias())


def _attn_sample_kernel(q_ref, kn_ref, vn_ref, kc_ref, vc_ref, cntc_ref, cntn_ref,
                        o_ref, ko_ref, vo_ref, *, ts, w_buf):
    rows = N_HEADS * ts
    ri = lax.broadcasted_iota(jnp.int32, (rows, ATT_W), 0)
    cj = lax.broadcasted_iota(jnp.int32, (rows, ATT_W), 1)
    own = (ri // ts) == (cj // HEAD_DIM)
    q_all = jnp.concatenate([q_ref[...] * (HEAD_DIM ** -0.5)] * N_HEADS, axis=0)
    q_st = jnp.where(own, q_all, 0.0).astype(BF16)
    kc = kc_ref[...]
    vc = vc_ref[...]
    kn = kn_ref[...]
    vn = vn_ref[...]
    cnt_c = cntc_ref[...]
    cnt_n = cntn_ref[...]
    s_c = jnp.where(cnt_c > 0.0, _mm(q_st, kc.astype(BF16), NT), NEG_INF)
    s_n = jnp.where(cnt_n > 0.0, _mm(q_st, kn.astype(BF16), NT), NEG_INF)
    m = jnp.maximum(jnp.max(s_c, axis=1, keepdims=True), jnp.max(s_n, axis=1, keepdims=True))
    p_c = cnt_c * jnp.exp(s_c - m)
    p_n = cnt_n * jnp.exp(s_n - m)
    den = jnp.sum(p_c, axis=1, keepdims=True) + jnp.sum(p_n, axis=1, keepdims=True)
    o_st = (_mm(p_c.astype(BF16), vc.astype(BF16)) + _mm(p_n.astype(BF16), vn.astype(BF16))) / den
    o_st = jnp.where(own, o_st, 0.0)
    out = o_st[0:ts]
    for h in range(1, N_HEADS):
        out = out + o_st[h * ts:(h + 1) * ts]
    o_ref[...] = out
    ko_ref[0:w_buf - ts, :] = kc[ts:w_buf]
    ko_ref[w_buf - ts:w_buf, :] = kn
    vo_ref[0:w_buf - ts, :] = vc[ts:w_buf]
    vo_ref[w_buf - ts:w_buf, :] = vn


def _branch_counts(ts, w_buf):
    delta = (w_buf + jnp.arange(ts)[:, None]) - jnp.arange(w_buf + ts)[None, :]
    cnt = jnp.zeros(delta.shape, F32)
    for d in DILATIONS:
        cnt = cnt + ((delta >= 0) & (delta % d == 0) & (delta // d <= N_STEPS)).astype(F32)
    cnt = jnp.tile(cnt, (N_HEADS, 1))
    return cnt[:, :w_buf], cnt[:, w_buf:]


def _attn_sample(q, kn, vn, kcache, vcache):
    b, ts, _ = q.shape
    w_buf = kcache.shape[1]
    assert w_buf >= N_STEPS * max(DILATIONS) and ts % SUBLANES == 0
    cnt_c, cnt_n = _branch_counts(ts, w_buf)
    new = pl.BlockSpec((None, ts, ATT_W), lambda i: (i, 0, 0))
    cache = pl.BlockSpec((None, w_buf, ATT_W), lambda i: (i, 0, 0))
    return pl.pallas_call(
        functools.partial(_attn_sample_kernel, ts=ts, w_buf=w_buf),
        grid=(b,),
        in_specs=[new, new, new, cache, cache,
                  _const_spec((N_HEADS * ts, w_buf)), _const_spec((N_HEADS * ts, ts))],
        out_specs=[new, cache, cache],
        out_shape=[jax.ShapeDtypeStruct((b, ts, ATT_W), F32),
                   jax.ShapeDtypeStruct((b, w_buf, ATT_W), F32),
                   jax.ShapeDtypeStruct((b, w_buf, ATT_W), F32)],
        compiler_params=_params(("parallel",)),
        name="attn_sample",
    )(q, kn, vn, kcache, vcache, cnt_c, cnt_n)


FFN_TF = 256
N_FCHUNK = D_FF // FFN_TF


def _mlp_kernel(*refs, tm, tiles_per_seq, seq_len, has_state):
    if has_state:
        (x_ref, rw_ref, att_ref, wout_ref, gmix_ref, gpre_ref, gpost_ref, wup_ref, cw_ref, cb_ref,
         wd_ref, p2_ref, y_ref, u_ref, act_scr) = refs
    else:
        (x_ref, rw_ref, att_ref, wout_ref, gmix_ref, gpre_ref, gpost_ref, wup_ref, cw_ref, cb_ref,
         wd_ref, y_ref, u_ref, act_scr, ubuf_scr, carry_scr) = refs

        @pl.when((pl.program_id(0) % tiles_per_seq) == 0)
        def _():
            carry_scr[...] = jnp.zeros_like(carry_scr)

    mix = (_mm(rw_ref[...].astype(BF16), wout_ref[:RWKV_W, :])
           + _mm(att_ref[...].astype(BF16), wout_ref[RWKV_W:, :]))
    x1 = x_ref[...] + _rms(mix, gmix_ref[...])
    hb = _rms(x1, gpre_ref[...]).astype(BF16)
    t_in = lax.broadcasted_iota(jnp.int32, (tm, 1), 0) % seq_len

    for c in range(N_FCHUNK):
        conv = []
        for idx, col0 in enumerate((c * FFN_TF, D_FF + c * FFN_TF)):
            cols = slice(col0, col0 + FFN_TF)
            u = _mm(hb, wup_ref[:, cols])
            if has_state:
                p2 = p2_ref[:, cols]
                u1 = jnp.where(t_in >= 1, pltpu.roll(u, 1, 0), pltpu.roll(p2, tm - 1, 0))
                u2 = jnp.where(t_in >= 2, pltpu.roll(u, 2, 0), p2)
                u_ref[:, cols] = u
            else:
                stage = ubuf_scr.at[c % 2, idx]
                stage[0:SUBLANES, :] = carry_scr[c, idx]
                stage[SUBLANES:, :] = u
                u1 = stage[SUBLANES - 1:SUBLANES - 1 + tm, :]
                u2 = stage[SUBLANES - 2:SUBLANES - 2 + tm, :]
                carry_scr[c, idx] = u[tm - SUBLANES:]
                u_ref[:, cols] = u[tm - SUBLANES:]
            conv.append(u2 * cw_ref[0:1, cols] + u1 * cw_ref[1:2, cols] + u * cw_ref[2:3, cols]
                        + cb_ref[:, cols])
        act_scr[:, c * FFN_TF:(c + 1) * FFN_TF] = (conv[0] * _sigmoid(conv[0]) * conv[1]).astype(BF16)

    y_ref[...] = x1 + _rms(_mm(act_scr[...], wd_ref[...]), gpost_ref[...])


def _mlp(x2d, rw, att, w, tm, seq_len, state_rows=None):
    n = x2d.shape[0]
    has_state = state_rows is not None
    tiles_per_seq = max(seq_len // tm, 1)
    n_tiles = n // tm
    row = lambda i: (i, 0)
    vec = lambda: _const_spec((1, D_MODEL))
    in_specs = [pl.BlockSpec((tm, D_MODEL), row), pl.BlockSpec((tm, RWKV_W), row),
                pl.BlockSpec((tm, ATT_W), row), _const_spec((D_MODEL, D_MODEL)), vec(), vec(), vec(),
                _const_spec((D_MODEL, 2 * D_FF)), _const_spec((CONV_W, 2 * D_FF)),
                _const_spec((1, 2 * D_FF)), _const_spec((D_FF, D_MODEL))]
    args = [x2d, rw, att, w["w_out"], w["norm_mix_post"], w["norm_ffn_pre"], w["norm_ffn_post"],
            w["w_ffn_up"], w["ffn_conv_w"], w["ffn_conv_b"], w["w_ffn_down"]]
    scratch = [pltpu.VMEM((tm, D_FF), BF16)]
    if has_state:
        assert tm % seq_len == 0 and seq_len == SUBLANES
        in_specs.append(pl.BlockSpec((tm, 2 * D_FF), row))
        args.append(state_rows)
        u_spec = pl.BlockSpec((tm, 2 * D_FF), row)
        u_shape = jax.ShapeDtypeStruct((n, 2 * D_FF), F32)
    else:
        assert seq_len % tm == 0
        scratch += [pltpu.VMEM((2, 2, tm + SUBLANES, FFN_TF), F32),
                    pltpu.VMEM((N_FCHUNK, 2, SUBLANES, FFN_TF), F32)]
        u_spec = pl.BlockSpec((None, SUBLANES, 2 * D_FF), lambda i: (i, 0, 0))
        u_shape = jax.ShapeDtypeStruct((n_tiles, SUBLANES, 2 * D_FF), F32)
    kern = functools.partial(_mlp_kernel, tm=tm, tiles_per_seq=tiles_per_seq, seq_len=seq_len,
                             has_state=has_state)
    return pl.pallas_call(
        kern,
        grid=(n_tiles,),
        in_specs=in_specs,
        out_specs=[pl.BlockSpec((tm, D_MODEL), row), u_spec],
        out_shape=[jax.ShapeDtypeStruct((n, D_MODEL), F32), u_shape],
        scratch_shapes=scratch,
        compiler_params=_params(("arbitrary",)),
        name="mlp",
    )(*args)


def _rope_tables(pos):
    half = HEAD_DIM // 2
    inv = ROPE_THETA ** (-jnp.arange(half, dtype=F32) / half)
    ang = pos.astype(F32)[:, None] * inv[None, :]
    cos, sin = jnp.cos(ang), jnp.sin(ang)
    cos_t = jnp.tile(jnp.concatenate([cos, cos], axis=1), (1, LANES // HEAD_DIM))
    sin_t = jnp.tile(jnp.concatenate([-sin, sin], axis=1), (1, LANES // HEAD_DIM))
    return cos_t, sin_t


def _layer(x, pos, h_prev, wkv0, k_buf, v_buf, conv_prev, w, tm, chunk, mlp_tm):
    b, t, _ = x.shape
    n = b * t
    x2d = x.reshape(n, D_MODEL)
    cos_t, sin_t = _rope_tables(pos)
    if t < tm:
        cos_t = jnp.tile(cos_t, (tm // t, 1))
        sin_t = jnp.tile(sin_t, (tm // t, 1))
        pos_tiles = 1
    else:
        pos_tiles = t // tm
    keep = min(N_STEPS * max(DILATIONS), t)
    tail_tiles = keep // tm if k_buf is None else 0
    rw, qkv, *tails = _in_proj(x2d, w["norm_mix_pre"], w["w_in"], cos_t, sin_t, tm, pos_tiles, tail_tiles)
    h_last, pstart = _shift_state(x[:, -1], w["norm_mix_pre"], h_prev, w["w_in_rw"])

    rwkv_out, wkv_new = _rwkv(rw.reshape(b, t, RWKV_COLS), pstart.reshape(b, 1, RWKV_COLS), wkv0,
                              w["rwkv"], chunk)
    qkv3 = qkv.reshape(b, t, 3 * ATT_W)
    if k_buf is None:
        att = _attn_prompt(qkv3)
        k_new, v_new = tails
    else:
        w_buf = k_buf.shape[1]
        q3, k3, v3 = (qkv3[:, :, ATT_W * i:ATT_W * (i + 1)] for i in range(3))
        att, k_new, v_new = _attn_sample(q3, k3, v3, k_buf.reshape(b, w_buf, ATT_W),
                                         v_buf.reshape(b, w_buf, ATT_W))
    rw2d, att2d = rwkv_out.reshape(n, RWKV_W), att.reshape(n, ATT_W)
    if conv_prev is None:
        y, tails = _mlp(x2d, rw2d, att2d, w, mlp_tm, t)
        conv_new = tails.reshape(b, t // mlp_tm, SUBLANES, 2 * D_FF)[:, -1, SUBLANES - (CONV_W - 1):]
    else:
        p2 = jnp.pad(conv_prev, ((0, 0), (0, t - (CONV_W - 1)), (0, 0))).reshape(n, 2 * D_FF)
        y, u = _mlp(x2d, rw2d, att2d, w, mlp_tm, t, state_rows=p2)
        conv_new = u.reshape(b, t, 2 * D_FF)[:, t - (CONV_W - 1):]
    heads = lambda a: a.reshape(b, a.shape[1], N_HEADS, HEAD_DIM)
    return y.reshape(b, t, D_MODEL), h_last, wkv_new, heads(k_new), heads(v_new), conv_new


def _prep_weights(norm_mix_pre, norm_mix_post, norm_ffn_pre, norm_ffn_post, w_in, mu_shift, w0,
                  w_decay_up, a0, w_iclr_up, w_gate_up, k_k, k_a, r_k, lnx_w, lnx_b, w_out,
                  w_ffn_up, ffn_conv_w, ffn_conv_b, w_ffn_down):
    vec = lambda a: a.reshape(1, -1)
    zero = jnp.zeros((D_DECAY, RWKV_W), F32)
    lora = jnp.concatenate([jnp.concatenate([w_decay_up, zero], axis=1),
                            jnp.concatenate([zero, w_iclr_up], axis=1)], axis=0)
    lora_hi, lora_lo = _split(lora)
    wg_hi, wg_lo = _split(w_gate_up)
    w_in_bf = w_in.astype(BF16)
    return {
        "norm_mix_pre": vec(norm_mix_pre), "norm_mix_post": vec(norm_mix_post),
        "norm_ffn_pre": vec(norm_ffn_pre), "norm_ffn_post": vec(norm_ffn_post),
        "w_in": w_in_bf, "w_in_rw": w_in_bf[:, :RWKV_COLS],
        "rwkv": (vec(mu_shift), vec(w0), vec(a0), lora_hi, lora_lo, wg_hi, wg_lo,
                 vec(k_k), vec(k_a), vec(r_k), vec(lnx_w), vec(lnx_b)),
        "w_out": w_out.astype(BF16), "w_ffn_up": w_ffn_up.astype(BF16),
        "ffn_conv_w": ffn_conv_w, "ffn_conv_b": vec(ffn_conv_b), "w_ffn_down": w_ffn_down.astype(BF16),
    }


PROMPT_TM = 512
PROMPT_CHUNK = 64
SAMPLE_MLP_TM = 128


def kernel(x_prompt, x_sample, state_rwkv_shift, state_rwkv_wkv, cache_att_k, cache_att_v, state_ffn_conv, norm_mix_pre, norm_mix_post, norm_ffn_pre, norm_ffn_post, w_in, mu_shift, w0, w_decay_up, a0, w_iclr_up, w_gate_up, k_k, k_a, r_k, lnx_w, lnx_b, w_out, w_ffn_up, ffn_conv_w, ffn_conv_b, w_ffn_down):
    bp, tp, _ = x_prompt.shape
    bs, ts, _ = x_sample.shape
    depth = norm_mix_pre.shape[0]
    pos_p = jnp.arange(tp, dtype=jnp.int32)
    pos_s = PAST_LEN + jnp.arange(ts, dtype=jnp.int32)
    yp, ys = x_prompt, x_sample
    outs_p = [[] for _ in range(5)]
    outs_s = [[] for _ in range(5)]
    for l in range(depth):
        w = _prep_weights(norm_mix_pre[l], norm_mix_post[l], norm_ffn_pre[l], norm_ffn_post[l], w_in[l],
                          mu_shift[l], w0[l], w_decay_up[l], a0[l], w_iclr_up[l], w_gate_up[l], k_k[l],
                          k_a[l], r_k[l], lnx_w[l], lnx_b[l], w_out[l], w_ffn_up[l], ffn_conv_w[l],
                          ffn_conv_b[l], w_ffn_down[l])
        yp, *state_p = _layer(yp, pos_p, jnp.zeros((bp, D_MODEL), F32),
                              jnp.zeros((bp, N_HEADS, HEAD_DIM, HEAD_DIM), F32), None, None, None,
                              w, min(PROMPT_TM, tp), min(PROMPT_CHUNK, tp), min(PROMPT_TM, tp))
        ys, *state_s = _layer(ys, pos_s, state_rwkv_shift[l], state_rwkv_wkv[l], cache_att_k[l],
                              cache_att_v[l], state_ffn_conv[l], w, bs * ts, ts, SAMPLE_MLP_TM)
        for acc, val in zip(outs_p, state_p):
            acc.append(val)
        for acc, val in zip(outs_s, state_s):
            acc.append(val)
    return (yp, ys, *(jnp.stack(a) for a in outs_p), *(jnp.stack(a) for a in outs_s))
```

```python
import functools

import jax
import jax.numpy as jnp
from jax import lax
from jax.experimental import pallas as pl
from jax.experimental.pallas import tpu as pltpu

F32 = jnp.float32
BF16 = jnp.bfloat16

D_MODEL = 1024
HEAD_DIM = 64
RWKV_W = 512
ATT_W = 512
N_HEADS = 8
N_PAIRS = N_HEADS // 2
D_DECAY = 64
D_ICLR = 64
D_GATE = 128
RWKV_COLS = 3 * RWKV_W + D_DECAY + D_ICLR + D_GATE
D_IN = RWKV_COLS + 3 * ATT_W
DILATIONS = (1, 4, 16)
N_STEPS = 128
ROPE_THETA = 10000.0
D_FF = 2816
CONV_W = 3
NORM_EPS = 1e-6
LNX_EPS = 64e-5
NEG_INF = -1e30
PAST_LEN = 16384

LANES = 128
SUBLANES = 8
VMEM_LIMIT = 56 * 1024 * 1024

NN = (((1,), (0,)), ((), ()))
NT = (((1,), (1,)), ((), ()))
TN = (((0,), (0,)), ((), ()))


def _mm(a, b, dims=NN):
    return lax.dot_general(a, b, dims, preferred_element_type=F32)


def _split(x):
    hi = x.astype(BF16)
    lo = (x - hi.astype(F32)).astype(BF16)
    return hi, lo


def _mm1(a, b, dims=NN):
    return _mm(a.astype(BF16), b.astype(BF16), dims)


_mm_gram = _mm1
_mm_inv = _mm1
_mm_apply = _mm1
_mm_state = _mm1


def _mm_split_lhs(a, b_bf16):
    ah, al = _split(a)
    return _mm(ah, b_bf16) + _mm(al, b_bf16)


def _mm_split_lhs_rhs(a_bf16, b):
    bh, bl = _split(b)
    return _mm(a_bf16, bh) + _mm(a_bf16, bl)


def _rms(x, g):
    return x * lax.rsqrt(jnp.mean(x * x, axis=-1, keepdims=True) + NORM_EPS) * g


def _sigmoid(x):
    return 1.0 / (1.0 + jnp.exp(-x))


def _sigmoid_tanh(x):
    return 0.5 + 0.5 * jnp.tanh(0.5 * x)


EXP_M_HALF = 0.6065306597126334


def _params(sem):
    return pltpu.CompilerParams(dimension_semantics=sem, vmem_limit_bytes=VMEM_LIMIT)


def _const_spec(shape):
    nd = len(shape)
    return pl.BlockSpec(shape, lambda *_: (0,) * nd, pipeline_mode=pl.Buffered(1))


def _in_proj_kernel(x_ref, g_ref, w_ref, cos_ref, sin_ref, rw_ref, qkv_ref, *tail_refs, pos_tiles):
    lane = lax.broadcasted_iota(jnp.int32, (1, ATT_W), 1)
    first_half = (lane % HEAD_DIM) < HEAD_DIM // 2
    q0 = RWKV_COLS
    tm = x_ref.shape[0]
    hb = _rms(x_ref[...], g_ref[...]).astype(BF16)
    rw_ref[...] = _mm(hb, w_ref[:, :RWKV_COLS]).astype(BF16)
    pos_rows = pl.ds(pl.multiple_of((pl.program_id(0) % pos_tiles) * tm, tm), tm)
    cos = jnp.concatenate([cos_ref[pos_rows, :]] * (ATT_W // LANES), axis=1)
    sin = jnp.concatenate([sin_ref[pos_rows, :]] * (ATT_W // LANES), axis=1)

    def rope(t):
        partner = jnp.where(first_half, pltpu.roll(t, ATT_W - HEAD_DIM // 2, 1),
                            pltpu.roll(t, HEAD_DIM // 2, 1))
        return t * cos + partner * sin

    qkv_ref[:, 0:ATT_W] = rope(_mm(hb, w_ref[:, q0:q0 + ATT_W]))
    k = rope(_mm(hb, w_ref[:, q0 + ATT_W:q0 + 2 * ATT_W]))
    v = _mm(hb, w_ref[:, q0 + 2 * ATT_W:q0 + 3 * ATT_W])
    qkv_ref[:, ATT_W:2 * ATT_W] = k
    qkv_ref[:, 2 * ATT_W:3 * ATT_W] = v
    if tail_refs:
        tail_refs[0][...] = k
        tail_refs[1][...] = v


def _in_proj(x2d, g, w_bf, cos_t, sin_t, tm, pos_tiles, tail_tiles=0):
    n = x2d.shape[0]
    row = lambda i: (i, 0)
    out_specs = [pl.BlockSpec((tm, RWKV_COLS), row), pl.BlockSpec((tm, 3 * ATT_W), row)]
    out_shape = [jax.ShapeDtypeStruct((n, RWKV_COLS), BF16), jax.ShapeDtypeStruct((n, 3 * ATT_W), F32)]
    if tail_tiles:
        skip = pos_tiles - tail_tiles
        tail = lambda i: (i // pos_tiles, jnp.maximum(i % pos_tiles - skip, 0), 0)
        out_specs += [pl.BlockSpec((None, tm, ATT_W), tail)] * 2
        out_shape += [jax.ShapeDtypeStruct((n // (pos_tiles * tm), tail_tiles * tm, ATT_W), F32)] * 2
    table = _const_spec((pos_tiles * tm, LANES))
    return pl.pallas_call(
        functools.partial(_in_proj_kernel, pos_tiles=pos_tiles),
        grid=(n // tm,),
        in_specs=[pl.BlockSpec((tm, D_MODEL), row), _const_spec((1, D_MODEL)),
                  _const_spec((D_MODEL, D_IN)), table, table],
        out_specs=out_specs,
        out_shape=out_shape,
        compiler_params=_params(("arbitrary",)),
        name="in_proj",
    )(x2d, g, w_bf, cos_t, sin_t)


def _shift_state_kernel(x_ref, g_ref, h_ref, w_ref, hl_ref, ps_ref):
    hl_ref[...] = _rms(x_ref[...], g_ref[...])
    ps_ref[...] = _mm(h_ref[...].astype(BF16), w_ref[...])


def _shift_state(x_last, g, h_prev, w_rw_bf):
    b = x_last.shape[0]
    return pl.pallas_call(
        _shift_state_kernel,
        out_shape=[jax.ShapeDtypeStruct((b, D_MODEL), F32), jax.ShapeDtypeStruct((b, RWKV_COLS), F32)],
        compiler_params=_params(None),
        name="shift_state",
    )(x_last, g, h_prev, w_rw_bf)


def _rwkv_kernel(p_ref, ps_ref, s0_ref, mu_ref, w0_ref, a0_ref, lora_hi_ref, lora_lo_ref,
                 wg_hi_ref, wg_lo_ref, kk_ref, ka_ref, rk_ref, lnw_ref, lnb_ref,
                 y_ref, sfin_ref, s_scr, prev_scr, pre_scr, bonus_scr, *, C, n_chunks, nb):
    ci = pl.program_id(1)
    C2 = 2 * C
    rows = nb * C
    seqs = range(nb)
    rsl = [slice(C * s, C * (s + 1)) for s in seqs]

    @pl.when(ci == 0)
    def _():
        prev_scr[...] = ps_ref[...]
        z = jnp.zeros((HEAD_DIM, HEAD_DIM), F32)
        for s in seqs:
            for p in range(N_PAIRS):
                top = jnp.concatenate([s0_ref[s, 2 * p], z], axis=1)
                bot = jnp.concatenate([z, s0_ref[s, 2 * p + 1]], axis=1)
                s_scr[s * N_PAIRS + p] = jnp.concatenate([top, bot], axis=0)

    lane = lax.broadcasted_iota(jnp.int32, (1, LANES), 1)
    lane_lo = lane < HEAD_DIM
    ri = lax.broadcasted_iota(jnp.int32, (LANES, LANES), 0)
    cj = lax.broadcasted_iota(jnp.int32, (LANES, LANES), 1)
    seg_ones = jnp.where((ri < HEAD_DIM) == (cj < HEAD_DIM), 1.0, 0.0).astype(BF16)
    row_c = lax.broadcasted_iota(jnp.int32, (C, 1), 0)
    tri = jnp.where(lax.broadcasted_iota(jnp.int32, (C, C), 0) >= lax.broadcasted_iota(jnp.int32, (C, C), 1),
                    1.0, 0.0).astype(BF16)
    mid = C // 2 - 1

    def mixed_all(col0, width):
        cols = slice(col0, col0 + width)
        cur = p_ref[:, :, cols].astype(F32).reshape(rows, width)
        prev = jnp.concatenate([jnp.broadcast_to(prev_scr[s, :, cols], (C, width)) for s in seqs], axis=0)
        row_all = lax.broadcasted_iota(jnp.int32, (rows, 1), 0) % C
        shifted = jnp.where(row_all == 0, prev, pltpu.roll(cur, 1, 0))
        for s in seqs:
            prev_scr[s, :, cols] = cur[C * (s + 1) - 1:C * (s + 1), :]
        return cur + (shifted - cur) * mu_ref[:, cols]

    xl = mixed_all(3 * RWKV_W, LANES)
    gl = mixed_all(3 * RWKV_W + LANES, D_GATE)
    th, tl = _split(jnp.where(lane_lo, jnp.tanh(xl), xl))
    pre_scr[:, PRE_W:PRE_W + 2 * RWKV_W] = (_mm(th, lora_hi_ref[...])
                                            + (_mm(th, lora_lo_ref[...]) + _mm(tl, lora_hi_ref[...])))
    sgh, sgl = _split(_sigmoid_tanh(gl))
    pre_scr[:, PRE_GATE:PRE_GATE + RWKV_W] = (_mm(sgh, wg_hi_ref[...])
                                              + (_mm(sgh, wg_lo_ref[...]) + _mm(sgl, wg_hi_ref[...])))
    k_all = mixed_all(RWKV_W, RWKV_W)
    pre_scr[:, PRE_K:PRE_K + RWKV_W] = k_all
    kkr = k_all * kk_ref[...]
    for p in range(N_PAIRS):
        sl = slice(LANES * p, LANES * (p + 1))
        pre_scr[:, PRE_SQ + LANES * p:PRE_SQ + LANES * (p + 1)] = _mm_split_lhs((kkr * kkr)[:, sl], seg_ones)

    def stack(x):
        return jnp.concatenate([jnp.where(lane_lo, x, 0.0), jnp.where(lane_lo, 0.0, x)], axis=0)

    def unit_inputs(s, p):
        def mixed(col0):
            cols = slice(col0, col0 + LANES)
            cur = p_ref[s, :, cols].astype(F32)
            shifted = jnp.where(row_c == 0, prev_scr[s, :, cols], pltpu.roll(cur, 1, 0))
            prev_scr[s, :, cols] = cur[C - 1:C, :]
            return cur + (shifted - cur) * mu_ref[:, cols]

        sl = slice(LANES * p, LANES * (p + 1))
        rs = rsl[s]
        pre = lambda base: pre_scr[rs, base + LANES * p:base + LANES * (p + 1)]
        r = mixed(LANES * p)
        v = mixed(2 * RWKV_W + LANES * p)
        k = pre(PRE_K)
        logd = -EXP_M_HALF * _sigmoid(w0_ref[:, sl] + pre(PRE_W))
        a_lr = _sigmoid_tanh(a0_ref[:, sl] + pre(PRE_W + RWKV_W))
        kkn = k * kk_ref[:, sl] / jnp.maximum(jnp.sqrt(pre(PRE_SQ)), 1e-12)
        kmod = k * (1.0 + (a_lr - 1.0) * ka_ref[:, sl])
        c = _mm_split_lhs_rhs(tri, logd)
        c_mid = c[mid:mid + 1, :]
        c_end = c[C - 1:C, :]
        cp = c - c_mid
        e_pos = jnp.exp(cp)
        e_neg = jnp.exp(-cp)
        e_prev = jnp.where(row_c == 0, jnp.exp(-c_mid), pltpu.roll(e_pos, 1, 0))
        bonus_scr[rs, sl] = _mm1(r * kmod * rk_ref[:, sl], seg_ones) * v
        return dict(ar=jnp.concatenate([stack(-kkn * e_prev), stack(r * e_pos)], axis=0),
                    bk=jnp.concatenate([stack(kkn * a_lr * e_neg), stack(kmod * e_neg)], axis=0),
                    v=stack(v), e_mid=jnp.exp(c_mid), e_end=jnp.exp(c_end), e_endp=jnp.exp(c_end - c_mid))

    ri2 = lax.broadcasted_iota(jnp.int32, (C2, C2), 0)
    cj2 = lax.broadcasted_iota(jnp.int32, (C2, C2), 1)
    strict = cj2 < ri2
    incl = cj2 <= ri2
    n_sq = C.bit_length() - 2
    fused = C2 % LANES == 0
    pairs = range(nb * N_PAIRS)
    sls = [slice(LANES * p, LANES * (p + 1)) for _ in seqs for p in range(N_PAIRS)]
    units = [unit_inputs(s, p) for s in seqs for p in range(N_PAIRS)]
    ar = [un["ar"] for un in units]
    bk = [un["bk"] for un in units]
    v_st = [un["v"] for un in units]
    e_mid = [un["e_mid"] for un in units]
    e_end = [un["e_end"] for un in units]
    e_endp = [un["e_endp"] for un in units]
    if fused:
        tri_mask = jnp.concatenate([jnp.concatenate([strict, strict], axis=1),
                                    jnp.concatenate([incl, incl], axis=1)], axis=0)
        gram = [jnp.where(tri_mask, _mm_gram(ar[p], bk[p], NT), 0.0) for p in pairs]
        g_ab = [g[:C2, :C2] for g in gram]
        g_ak = [g[:C2, C2:] for g in gram]
        g_rb = [g[C2:, :C2] for g in gram]
        g_rk = [g[C2:, C2:] for g in gram]
    else:
        g_ab = [jnp.where(strict, _mm_gram(ar[p][:C2], bk[p][:C2], NT), 0.0) for p in pairs]
        g_ak = [jnp.where(strict, _mm_gram(ar[p][:C2], bk[p][C2:], NT), 0.0) for p in pairs]
        g_rb = [jnp.where(incl, _mm_gram(ar[p][C2:], bk[p][:C2], NT), 0.0) for p in pairs]
        g_rk = [jnp.where(incl, _mm_gram(ar[p][C2:], bk[p][C2:], NT), 0.0) for p in pairs]

    z2 = [_mm_apply(g_ak[p], v_st[p]) for p in pairs]
    x = [jnp.concatenate([ar[p][:C2] * e_mid[p], z2[p]], axis=1) for p in pairs]
    lp = g_ab
    x = [x[p] + _mm_apply(lp[p], x[p]) for p in pairs]
    for _ in range(n_sq):
        lp = [_mm_inv(lp[p], lp[p]) for p in pairs]
        x = [x[p] + _mm_apply(lp[p], x[p]) for p in pairs]

    s_old = [s_scr[p] for p in pairs]
    wr = [jnp.concatenate([x[p][:, :LANES], ar[p][C2:] * e_mid[p]], axis=0) for p in pairs]
    ws = [_mm_state(wr[p], s_old[p], NT) for p in pairs]
    u = [ws[p][:C2] + x[p][:, LANES:] for p in pairs]
    uv = [jnp.concatenate([u[p], v_st[p]], axis=0) for p in pairs]
    if fused:
        y_st = [ws[p][C2:] + _mm_state(jnp.concatenate([g_rb[p], g_rk[p]], axis=1), uv[p]) for p in pairs]
    else:
        y_st = [ws[p][C2:] + _mm_state(g_rb[p], u[p]) + _mm_state(g_rk[p], v_st[p]) for p in pairs]
    for p in pairs:
        s_scr[p] = s_old[p] * e_end[p] + _mm_state(uv[p], bk[p] * e_endp[p], TN)

    y = [jnp.concatenate([y_st[s * N_PAIRS + p][:C] + y_st[s * N_PAIRS + p][C:] for s in seqs], axis=0)
         for p in range(N_PAIRS)]
    for p in range(N_PAIRS):
        sl = sls[p]
        dev = y[p] - _mm1(y[p], seg_ones) * (1.0 / HEAD_DIM)
        var = _mm1(dev * dev, seg_ones) * (1.0 / HEAD_DIM)
        yn = dev * lax.rsqrt(var + LNX_EPS) * lnw_ref[:, sl] + lnb_ref[:, sl]
        gate = pre_scr[:, PRE_GATE + LANES * p:PRE_GATE + LANES * (p + 1)]
        y_ref[:, :, sl] = ((yn + bonus_scr[:, sl]) * gate).reshape(nb, C, LANES).astype(y_ref.dtype)

    @pl.when(ci == n_chunks - 1)
    def _():
        for s in seqs:
            for p in range(N_PAIRS):
                s_fin = s_scr[s * N_PAIRS + p]
                sfin_ref[s, 2 * p] = s_fin[:HEAD_DIM, :HEAD_DIM]
                sfin_ref[s, 2 * p + 1] = s_fin[HEAD_DIM:, HEAD_DIM:]


RWKV_SEQS = 8
PRE_W = 0
PRE_GATE = 2 * RWKV_W
PRE_K = 3 * RWKV_W
PRE_SQ = 4 * RWKV_W
PRE_COLS = 5 * RWKV_W


def _rwkv(proj_rw, pstart, s0, wts, C):
    b, t, _ = proj_rw.shape
    nb = RWKV_SEQS
    assert b % nb == 0 and t % C == 0
    n_chunks = t // C
    vec = lambda n: _const_spec((1, n))
    kern = functools.partial(_rwkv_kernel, C=C, n_chunks=n_chunks, nb=nb)
    return pl.pallas_call(
        kern,
        grid=(b // nb, n_chunks),
        in_specs=[pl.BlockSpec((nb, C, RWKV_COLS), lambda i, j: (i, j, 0)),
                  pl.BlockSpec((nb, 1, RWKV_COLS), lambda i, j: (i, 0, 0)),
                  pl.BlockSpec((nb, N_HEADS, HEAD_DIM, HEAD_DIM), lambda i, j: (i, 0, 0, 0)),
                  vec(RWKV_COLS), vec(RWKV_W), vec(RWKV_W),
                  _const_spec((LANES, 2 * RWKV_W)), _const_spec((LANES, 2 * RWKV_W)),
                  _const_spec((D_GATE, RWKV_W)), _const_spec((D_GATE, RWKV_W)),
                  vec(RWKV_W), vec(RWKV_W), vec(RWKV_W), vec(RWKV_W), vec(RWKV_W)],
        out_specs=[pl.BlockSpec((nb, C, RWKV_W), lambda i, j: (i, j, 0)),
                   pl.BlockSpec((nb, N_HEADS, HEAD_DIM, HEAD_DIM), lambda i, j: (i, 0, 0, 0))],
        out_shape=[jax.ShapeDtypeStruct((b, t, RWKV_W), BF16 if C % (2 * SUBLANES) == 0 else F32),
                   jax.ShapeDtypeStruct((b, N_HEADS, HEAD_DIM, HEAD_DIM), F32)],
        scratch_shapes=[pltpu.VMEM((nb * N_PAIRS, LANES, LANES), F32), pltpu.VMEM((nb, 1, RWKV_COLS), F32),
                        pltpu.VMEM((nb * C, PRE_COLS), F32), pltpu.VMEM((nb * C, RWKV_W), F32)],
        compiler_params=_params(("parallel", "arbitrary")),
        name="rwkv",
    )(proj_rw, pstart, s0, *wts)


ATT_UNROLL = 4
LOG2E = 1.4426950408889634


def _attn_prompt_kernel(q_ref, k_ref, v_ref, bias_ref, o_ref, acc_scr, m_scr, l_scr, *, T):
    QB = N_STEPS
    lane = lax.broadcasted_iota(jnp.int32, (1, LANES), 1)
    lane_lo = lane < HEAD_DIM

    def halves(x):
        return jnp.where(lane_lo, x[:QB], x[QB:])

    order = sorted(DILATIONS, reverse=True)
    assert order[-1] == 1
    for bi, d in enumerate(order):
        per_res = T // (QB * d)
        ub = min(ATT_UNROLL, per_res)
        groups = ATT_UNROLL // ub
        gpr = per_res // ub

        def rows(start, d=d):
            if d == 1:
                return pl.ds(pl.multiple_of(start, QB), QB)
            return pl.ds(start, QB, stride=d)

        def body(it, carry, bi=bi, d=d, ub=ub, groups=groups, gpr=gpr, rows=rows):
            blocks = []
            for g in range(groups):
                gi = it * groups + g
                res = gi // gpr
                n0 = (gi % gpr) * ub
                starts = [res + d * QB * jnp.maximum(n0 - 1, 0)] + [res + d * QB * (n0 + u) for u in range(ub)]
                kb = [k_ref[rows(s), :].astype(BF16) for s in starts]
                vb = [v_ref[rows(s), :].astype(BF16) for s in starts]
                for u in range(ub):
                    bias = bias_ref[jnp.where(n0 == 0, 1, 0)] if u == 0 else bias_ref[0]
                    blocks.append((starts[u + 1], jnp.concatenate([kb[u], kb[u + 1]], axis=0),
                                   jnp.concatenate([vb[u], vb[u + 1]], axis=0), bias))
            q_st = []
            for start, _, _, _ in blocks:
                q = q_ref[rows(start), :] * (HEAD_DIM ** -0.5 * LOG2E)
                q_st.append(jnp.concatenate([jnp.where(lane_lo, q, 0.0), jnp.where(lane_lo, 0.0, q)],
                                            axis=0).astype(BF16))
            s = [_mm(q_st[i], blk[1], NT) + blk[3] for i, blk in enumerate(blocks)]
            m = [jnp.max(x, axis=1, keepdims=True) for x in s]
            p = [jnp.exp2(s[i] - m[i]) for i in range(len(blocks))]
            l = [jnp.sum(x, axis=1, keepdims=True) for x in p]
            o = [halves(_mm(p[i].astype(BF16), blk[2])) for i, blk in enumerate(blocks)]
            m_b = [halves(jnp.broadcast_to(x, (2 * QB, LANES))) for x in m]
            l_b = [halves(jnp.broadcast_to(x, (2 * QB, LANES))) for x in l]
            if bi == 0:
                for i, blk in enumerate(blocks):
                    acc_scr[rows(blk[0]), :] = o[i]
                    m_scr[rows(blk[0]), :] = m_b[i]
                    l_scr[rows(blk[0]), :] = l_b[i]
            else:
                m_old = [m_scr[rows(blk[0]), :] for blk in blocks]
                acc_old = [acc_scr[rows(blk[0]), :] for blk in blocks]
                l_old = [l_scr[rows(blk[0]), :] for blk in blocks]
                for i, blk in enumerate(blocks):
                    m_new = jnp.maximum(m_old[i], m_b[i])
                    w_old = jnp.exp2(m_old[i] - m_new)
                    w_new = jnp.exp2(m_b[i] - m_new)
                    acc = acc_old[i] * w_old + o[i] * w_new
                    den = l_old[i] * w_old + l_b[i] * w_new
                    if bi < len(order) - 1:
                        acc_scr[rows(blk[0]), :] = acc
                        m_scr[rows(blk[0]), :] = m_new
                        l_scr[rows(blk[0]), :] = den
                    else:
                        o_ref[rows(blk[0]), :] = (acc / den).astype(o_ref.dtype)
            return carry

        lax.fori_loop(0, d * gpr // groups, body, 0)


def _band_bias():
    qi = jnp.arange(2 * N_STEPS)[:, None] % N_STEPS
    kj = jnp.arange(2 * N_STEPS)[None, :]
    band = (kj >= qi) & (kj <= qi + N_STEPS)
    normal = jnp.where(band, 0.0, NEG_INF).astype(F32)
    first = jnp.where(band & (kj >= N_STEPS), 0.0, NEG_INF).astype(F32)
    return jnp.stack([normal, first])


def _attn_prompt(qkv):
    b, t, _ = qkv.shape
    assert t % (N_STEPS * max(DILATIONS)) == 0
    spec = pl.BlockSpec((None, t, LANES), lambda i, p: (i, 0, p))
    slab = lambda j: pl.BlockSpec((None, t, LANES), lambda i, p, j=j: (i, 0, j * N_PAIRS + p))
    q, k, v = qkv, qkv, qkv
    return pl.pallas_call(
        functools.partial(_attn_prompt_kernel, T=t),
        grid=(b, N_PAIRS),
        in_specs=[slab(0), slab(1), slab(2), _const_spec((2, 2 * N_STEPS, 2 * N_STEPS))],
        out_specs=spec,
        out_shape=jax.ShapeDtypeStruct((b, t, ATT_W), BF16),
        scratch_shapes=[pltpu.VMEM((t, LANES), F32)] * 3,
        compiler_params=_params(("parallel", "parallel")),
        name="attn_prompt",
    )(q, k, v, _band_bias())


def _attn_sample_kernel(q_ref, kn_ref, vn_ref, kc_ref, vc_ref, cntc_ref, cntn_ref,
                        o_ref, ko_ref, vo_ref, *, ts, w_buf):
    rows = N_HEADS * ts
    ri = lax.broadcasted_iota(jnp.int32, (rows, ATT_W), 0)
    cj = lax.broadcasted_iota(jnp.int32, (rows, ATT_W), 1)
    own = (ri // ts) == (cj // HEAD_DIM)
    q_all = jnp.concatenate([q_ref[...] * (HEAD_DIM ** -0.5)] * N_HEADS, axis=0)
    q_st = jnp.where(own, q_all, 0.0).astype(BF16)
    kc = kc_ref[...]
    vc = vc_ref[...]
    kn = kn_ref[...]
    vn = vn_ref[...]
    cnt_c = cntc_ref[...]
    cnt_n = cntn_ref[...]
    s_c = jnp.where(cnt_c > 0.0, _mm(q_st, kc.astype(BF16), NT), NEG_INF)
    s_n = jnp.where(cnt_n > 0.0, _mm(q_st, kn.astype(BF16), NT), NEG_INF)
    m = jnp.maximum(jnp.max(s_c, axis=1, keepdims=True), jnp.max(s_n, axis=1, keepdims=True))
    p_c = cnt_c * jnp.exp(s_c - m)
    p_n = cnt_n * jnp.exp(s_n - m)
    den = jnp.sum(p_c, axis=1, keepdims=True) + jnp.sum(p_n, axis=1, keepdims=True)
    o_st = (_mm(p_c.astype(BF16), vc.astype(BF16)) + _mm(p_n.astype(BF16), vn.astype(BF16))) / den
    o_st = jnp.where(own, o_st, 0.0)
    out = o_st[0:ts]
    for h in range(1, N_HEADS):
        out = out + o_st[h * ts:(h + 1) * ts]
    o_ref[...] = out
    ko_ref[0:w_buf - ts, :] = kc[ts:w_buf]
    ko_ref[w_buf - ts:w_buf, :] = kn
    vo_ref[0:w_buf - ts, :] = vc[ts:w_buf]
    vo_ref[w_buf - ts:w_buf, :] = vn


def _branch_counts(ts, w_buf):
    delta = (w_buf + jnp.arange(ts)[:, None]) - jnp.arange(w_buf + ts)[None, :]
    cnt = jnp.zeros(delta.shape, F32)
    for d in DILATIONS:
        cnt = cnt + ((delta >= 0) & (delta % d == 0) & (delta // d <= N_STEPS)).astype(F32)
    cnt = jnp.tile(cnt, (N_HEADS, 1))
    return cnt[:, :w_buf], cnt[:, w_buf:]


def _attn_sample(q, kn, vn, kcache, vcache):
    b, ts, _ = q.shape
    w_buf = kcache.shape[1]
    assert w_buf >= N_STEPS * max(DILATIONS) and ts % SUBLANES == 0
    cnt_c, cnt_n = _branch_counts(ts, w_buf)
    new = pl.BlockSpec((None, ts, ATT_W), lambda i: (i, 0, 0))
    cache = pl.BlockSpec((None, w_buf, ATT_W), lambda i: (i, 0, 0))
    return pl.pallas_call(
        functools.partial(_attn_sample_kernel, ts=ts, w_buf=w_buf),
        grid=(b,),
        in_specs=[new, new, new, cache, cache,
                  _const_spec((N_HEADS * ts, w_buf)), _const_spec((N_HEADS * ts, ts))],
        out_specs=[new, cache, cache],
        out_shape=[jax.ShapeDtypeStruct((b, ts, ATT_W), F32),
                   jax.ShapeDtypeStruct((b, w_buf, ATT_W), F32),
                   jax.ShapeDtypeStruct((b, w_buf, ATT_W), F32)],
        compiler_params=_params(("parallel",)),
        name="attn_sample",
    )(q, kn, vn, kcache, vcache, cnt_c, cnt_n)


FFN_TF = 256
N_FCHUNK = D_FF // FFN_TF


def _mlp_kernel(*refs, tm, tiles_per_seq, seq_len, has_state):
    if has_state:
        (x_ref, rw_ref, att_ref, wout_ref, gmix_ref, gpre_ref, gpost_ref, wup_ref, cw_ref, cb_ref,
         wd_ref, p2_ref, y_ref, u_ref, act_scr) = refs
    else:
        (x_ref, rw_ref, att_ref, wout_ref, gmix_ref, gpre_ref, gpost_ref, wup_ref, cw_ref, cb_ref,
         wd_ref, y_ref, u_ref, act_scr, ubuf_scr, carry_scr) = refs

        @pl.when((pl.program_id(0) % tiles_per_seq) == 0)
        def _():
            carry_scr[...] = jnp.zeros_like(carry_scr)

    mix = (_mm(rw_ref[...].astype(BF16), wout_ref[:RWKV_W, :])
           + _mm(att_ref[...].astype(BF16), wout_ref[RWKV_W:, :]))
    x1 = x_ref[...] + _rms(mix, gmix_ref[...])
    hb = _rms(x1, gpre_ref[...]).astype(BF16)
    t_in = lax.broadcasted_iota(jnp.int32, (tm, 1), 0) % seq_len

    for c in range(N_FCHUNK):
        conv = []
        for idx, col0 in enumerate((c * FFN_TF, D_FF + c * FFN_TF)):
            cols = slice(col0, col0 + FFN_TF)
            u = _mm(hb, wup_ref[:, cols])
            if has_state:
                p2 = p2_ref[:, cols]
                u1 = jnp.where(t_in >= 1, pltpu.roll(u, 1, 0), pltpu.roll(p2, tm - 1, 0))
                u2 = jnp.where(t_in >= 2, pltpu.roll(u, 2, 0), p2)
                u_ref[:, cols] = u
            else:
                stage = ubuf_scr.at[c % 2, idx]
                stage[0:SUBLANES, :] = carry_scr[c, idx]
                stage[SUBLANES:, :] = u
                u1 = stage[SUBLANES - 1:SUBLANES - 1 + tm, :]
                u2 = stage[SUBLANES - 2:SUBLANES - 2 + tm, :]
                carry_scr[c, idx] = u[tm - SUBLANES:]
                u_ref[:, cols] = u[tm - SUBLANES:]
            conv.append(u2 * cw_ref[0:1, cols] + u1 * cw_ref[1:2, cols] + u * cw_ref[2:3, cols]
                        + cb_ref[:, cols])
        act_scr[:, c * FFN_TF:(c + 1) * FFN_TF] = (conv[0] * _sigmoid(conv[0]) * conv[1]).astype(BF16)

    y_ref[...] = x1 + _rms(_mm(act_scr[...], wd_ref[...]), gpost_ref[...])


def _mlp(x2d, rw, att, w, tm, seq_len, state_rows=None):
    n = x2d.shape[0]
    has_state = state_rows is not None
    tiles_per_seq = max(seq_len // tm, 1)
    n_tiles = n // tm
    row = lambda i: (i, 0)
    vec = lambda: _const_spec((1, D_MODEL))
    in_specs = [pl.BlockSpec((tm, D_MODEL), row), pl.BlockSpec((tm, RWKV_W), row),
                pl.BlockSpec((tm, ATT_W), row), _const_spec((D_MODEL, D_MODEL)), vec(), vec(), vec(),
                _const_spec((D_MODEL, 2 * D_FF)), _const_spec((CONV_W, 2 * D_FF)),
                _const_spec((1, 2 * D_FF)), _const_spec((D_FF, D_MODEL))]
    args = [x2d, rw, att, w["w_out"], w["norm_mix_post"], w["norm_ffn_pre"], w["norm_ffn_post"],
            w["w_ffn_up"], w["ffn_conv_w"], w["ffn_conv_b"], w["w_ffn_down"]]
    scratch = [pltpu.VMEM((tm, D_FF), BF16)]
    if has_state:
        assert tm % seq_len == 0 and seq_len == SUBLANES
        in_specs.append(pl.BlockSpec((tm, 2 * D_FF), row))
        args.append(state_rows)
        u_spec = pl.BlockSpec((tm, 2 * D_FF), row)
        u_shape = jax.ShapeDtypeStruct((n, 2 * D_FF), F32)
    else:
        assert seq_len % tm == 0
        scratch += [pltpu.VMEM((2, 2, tm + SUBLANES, FFN_TF), F32),
                    pltpu.VMEM((N_FCHUNK, 2, SUBLANES, FFN_TF), F32)]
        u_spec = pl.BlockSpec((None, SUBLANES, 2 * D_FF), lambda i: (i, 0, 0))
        u_shape = jax.ShapeDtypeStruct((n_tiles, SUBLANES, 2 * D_FF), F32)
    kern = functools.partial(_mlp_kernel, tm=tm, tiles_per_seq=tiles_per_seq, seq_len=seq_len,
                             has_state=has_state)
    return pl.pallas_call(
        kern,
        grid=(n_tiles,),
        in_specs=in_specs,
        out_specs=[pl.BlockSpec((tm, D_MODEL), row), u_spec],
        out_shape=[jax.ShapeDtypeStruct((n, D_MODEL), F32), u_shape],
        scratch_shapes=scratch,
        compiler_params=_params(("arbitrary",)),
        name="mlp",
    )(*args)


def _rope_tables(pos):
    half = HEAD_DIM // 2
    inv = ROPE_THETA ** (-jnp.arange(half, dtype=F32) / half)
    ang = pos.astype(F32)[:, None] * inv[None, :]
    cos, sin = jnp.cos(ang), jnp.sin(ang)
    cos_t = jnp.tile(jnp.concatenate([cos, cos], axis=1), (1, LANES // HEAD_DIM))
    sin_t = jnp.tile(jnp.concatenate([-sin, sin], axis=1), (1, LANES // HEAD_DIM))
    return cos_t, sin_t


def _layer(x, pos, h_prev, wkv0, k_buf, v_buf, conv_prev, w, tm, chunk, mlp_tm):
    b, t, _ = x.shape
    n = b * t
    x2d = x.reshape(n, D_MODEL)
    cos_t, sin_t = _rope_tables(pos)
    if t < tm:
        cos_t = jnp.tile(cos_t, (tm // t, 1))
        sin_t = jnp.tile(sin_t, (tm // t, 1))
        pos_tiles = 1
    else:
        pos_tiles = t // tm
    keep = min(N_STEPS * max(DILATIONS), t)
    tail_tiles = keep // tm if k_buf is None else 0
    rw, qkv, *tails = _in_proj(x2d, w["norm_mix_pre"], w["w_in"], cos_t, sin_t, tm, pos_tiles, tail_tiles)
    h_last, pstart = _shift_state(x[:, -1], w["norm_mix_pre"], h_prev, w["w_in_rw"])

    rwkv_out, wkv_new = _rwkv(rw.reshape(b, t, RWKV_COLS), pstart.reshape(b, 1, RWKV_COLS), wkv0,
                              w["rwkv"], chunk)
    qkv3 = qkv.reshape(b, t, 3 * ATT_W)
    if k_buf is None:
        att = _attn_prompt(qkv3)
        k_new, v_new = tails
    else:
        w_buf = k_buf.shape[1]
        q3, k3, v3 = (qkv3[:, :, ATT_W * i:ATT_W * (i + 1)] for i in range(3))
        att, k_new, v_new = _attn_sample(q3, k3, v3, k_buf.reshape(b, w_buf, ATT_W),
                                         v_buf.reshape(b, w_buf, ATT_W))
    rw2d, att2d = rwkv_out.reshape(n, RWKV_W), att.reshape(n, ATT_W)
    if conv_prev is None:
        y, tails = _mlp(x2d, rw2d, att2d, w, mlp_tm, t)
        conv_new = tails.reshape(b, t // mlp_tm, SUBLANES, 2 * D_FF)[:, -1, SUBLANES - (CONV_W - 1):]
    else:
        p2 = jnp.pad(conv_prev, ((0, 0), (0, t - (CONV_W - 1)), (0, 0))).reshape(n, 2 * D_FF)
        y, u = _mlp(x2d, rw2d, att2d, w, mlp_tm, t, state_rows=p2)
        conv_new = u.reshape(b, t, 2 * D_FF)[:, t - (CONV_W - 1):]
    heads = lambda a: a.reshape(b, a.shape[1], N_HEADS, HEAD_DIM)
    return y.reshape(b, t, D_MODEL), h_last, wkv_new, heads(k_new), heads(v_new), conv_new


def _prep_weights(norm_mix_pre, norm_mix_post, norm_ffn_pre, norm_ffn_post, w_in, mu_shift, w0,
                  w_decay_up, a0, w_iclr_up, w_gate_up, k_k, k_a, r_k, lnx_w, lnx_b, w_out,
                  w_ffn_up, ffn_conv_w, ffn_conv_b, w_ffn_down):
    vec = lambda a: a.reshape(1, -1)
    zero = jnp.zeros((D_DECAY, RWKV_W), F32)
    lora = jnp.concatenate([jnp.concatenate([w_decay_up, zero], axis=1),
                            jnp.concatenate([zero, w_iclr_up], axis=1)], axis=0)
    lora_hi, lora_lo = _split(lora)
    wg_hi, wg_lo = _split(w_gate_up)
    w_in_bf = w_in.astype(BF16)
    return {
        "norm_mix_pre": vec(norm_mix_pre), "norm_mix_post": vec(norm_mix_post),
        "norm_ffn_pre": vec(norm_ffn_pre), "norm_ffn_post": vec(norm_ffn_post),
        "w_in": w_in_bf, "w_in_rw": w_in_bf[:, :RWKV_COLS],
        "rwkv": (vec(mu_shift), vec(w0), vec(a0), lora_hi, lora_lo, wg_hi, wg_lo,
                 vec(k_k), vec(k_a), vec(r_k), vec(lnx_w), vec(lnx_b)),
        "w_out": w_out.astype(BF16), "w_ffn_up": w_ffn_up.astype(BF16),
        "ffn_conv_w": ffn_conv_w, "ffn_conv_b": vec(ffn_conv_b), "w_ffn_down": w_ffn_down.astype(BF16),
    }


PROMPT_TM = 512
PROMPT_CHUNK = 64
SAMPLE_MLP_TM = 128


def kernel(x_prompt, x_sample, state_rwkv_shift, state_rwkv_wkv, cache_att_k, cache_att_v, state_ffn_conv, norm_mix_pre, norm_mix_post, norm_ffn_pre, norm_ffn_post, w_in, mu_shift, w0, w_decay_up, a0, w_iclr_up, w_gate_up, k_k, k_a, r_k, lnx_w, lnx_b, w_out, w_ffn_up, ffn_conv_w, ffn_conv_b, w_ffn_down):
    bp, tp, _ = x_prompt.shape
    bs, ts, _ = x_sample.shape
    depth = norm_mix_pre.shape[0]
    pos_p = jnp.arange(tp, dtype=jnp.int32)
    pos_s = PAST_LEN + jnp.arange(ts, dtype=jnp.int32)
    yp, ys = x_prompt, x_sample
    outs_p = [[] for _ in range(5)]
    outs_s = [[] for _ in range(5)]
    for l in range(depth):
        w = _prep_weights(norm_mix_pre[l], norm_mix_post[l], norm_ffn_pre[l], norm_ffn_post[l], w_in[l],
                          mu_shift[l], w0[l], w_decay_up[l], a0[l], w_iclr_up[l], w_gate_up[l], k_k[l],
                          k_a[l], r_k[l], lnx_w[l], lnx_b[l], w_out[l], w_ffn_up[l], ffn_conv_w[l],
                          ffn_conv_b[l], w_ffn_down[l])
        yp, *state_p = _layer(yp, pos_p, jnp.zeros((bp, D_MODEL), F32),
                              jnp.zeros((bp, N_HEADS, HEAD_DIM, HEAD_DIM), F32), None, None, None,
                              w, min(PROMPT_TM, tp), min(PROMPT_CHUNK, tp), min(PROMPT_TM, tp))
        ys, *state_s = _layer(ys, pos_s, state_rwkv_shift[l], state_rwkv_wkv[l], cache_att_k[l],
                              cache_att_v[l], state_ffn_conv[l], w, bs * ts, ts, SAMPLE_MLP_TM)
        for acc, val in zip(outs_p, state_p):
            acc.append(val)
        for acc, val in zip(outs_s, state_s):
            acc.append(val)
    return (yp, ys, *(jnp.stack(a) for a in outs_p), *(jnp.stack(a) for a in outs_s))
```

```python
import functools

import jax
import jax.numpy as jnp
from jax import lax
from jax.experimental import pallas as pl
from jax.experimental.pallas import tpu as pltpu

F32 = jnp.float32
BF16 = jnp.bfloat16

D_MODEL = 1024
HEAD_DIM = 64
RWKV_W = 512
ATT_W = 512
N_HEADS = 8
N_PAIRS = N_HEADS // 2
D_DECAY = 64
D_ICLR = 64
D_GATE = 128
RWKV_COLS = 3 * RWKV_W + D_DECAY + D_ICLR + D_GATE
D_IN = RWKV_COLS + 3 * ATT_W
DILATIONS = (1, 4, 16)
N_STEPS = 128
ROPE_THETA = 10000.0
D_FF = 2816
CONV_W = 3
NORM_EPS = 1e-6
LNX_EPS = 64e-5
NEG_INF = -1e30
PAST_LEN = 16384

LANES = 128
SUBLANES = 8
VMEM_LIMIT = 56 * 1024 * 1024

NN = (((1,), (0,)), ((), ()))
NT = (((1,), (1,)), ((), ()))
TN = (((0,), (0,)), ((), ()))


def _mm(a, b, dims=NN):
    return lax.dot_general(a, b, dims, preferred_element_type=F32)


def _split(x):
    hi = x.astype(BF16)
    lo = (x - hi.astype(F32)).astype(BF16)
    return hi, lo


def _mm1(a, b, dims=NN):
    return _mm(a.astype(BF16), b.astype(BF16), dims)


_mm_gram = _mm1
_mm_inv = _mm1
_mm_apply = _mm1
_mm_state = _mm1


def _mm_split_lhs(a, b_bf16):
    ah, al = _split(a)
    return _mm(ah, b_bf16) + _mm(al, b_bf16)


def _mm_split_lhs_rhs(a_bf16, b):
    bh, bl = _split(b)
    return _mm(a_bf16, bh) + _mm(a_bf16, bl)


def _rms(x, g):
    return x * lax.rsqrt(jnp.mean(x * x, axis=-1, keepdims=True) + NORM_EPS) * g


def _sigmoid(x):
    return 1.0 / (1.0 + jnp.exp(-x))


def _sigmoid_tanh(x):
    return 0.5 + 0.5 * jnp.tanh(0.5 * x)


EXP_M_HALF = 0.6065306597126334


def _params(sem):
    return pltpu.CompilerParams(dimension_semantics=sem, vmem_limit_bytes=VMEM_LIMIT)


def _const_spec(shape):
    nd = len(shape)
    return pl.BlockSpec(shape, lambda *_: (0,) * nd, pipeline_mode=pl.Buffered(1))


def _in_proj_kernel(x_ref, g_ref, w_ref, cos_ref, sin_ref, rw_ref, qkv_ref, *tail_refs, pos_tiles):
    lane = lax.broadcasted_iota(jnp.int32, (1, ATT_W), 1)
    first_half = (lane % HEAD_DIM) < HEAD_DIM // 2
    q0 = RWKV_COLS
    tm = x_ref.shape[0]
    hb = _rms(x_ref[...], g_ref[...]).astype(BF16)
    rw_ref[...] = _mm(hb, w_ref[:, :RWKV_COLS]).astype(BF16)
    pos_rows = pl.ds(pl.multiple_of((pl.program_id(0) % pos_tiles) * tm, tm), tm)
    cos = jnp.concatenate([cos_ref[pos_rows, :]] * (ATT_W // LANES), axis=1)
    sin = jnp.concatenate([sin_ref[pos_rows, :]] * (ATT_W // LANES), axis=1)

    def rope(t):
        partner = jnp.where(first_half, pltpu.roll(t, ATT_W - HEAD_DIM // 2, 1),
                            pltpu.roll(t, HEAD_DIM // 2, 1))
        return t * cos + partner * sin

    qkv_ref[:, 0:ATT_W] = rope(_mm(hb, w_ref[:, q0:q0 + ATT_W]))
    k = rope(_mm(hb, w_ref[:, q0 + ATT_W:q0 + 2 * ATT_W]))
    v = _mm(hb, w_ref[:, q0 + 2 * ATT_W:q0 + 3 * ATT_W])
    qkv_ref[:, ATT_W:2 * ATT_W] = k
    qkv_ref[:, 2 * ATT_W:3 * ATT_W] = v
    if tail_refs:
        tail_refs[0][...] = k
        tail_refs[1][...] = v


def _in_proj(x2d, g, w_bf, cos_t, sin_t, tm, pos_tiles, tail_tiles=0):
    n = x2d.shape[0]
    row = lambda i: (i, 0)
    out_specs = [pl.BlockSpec((tm, RWKV_COLS), row), pl.BlockSpec((tm, 3 * ATT_W), row)]
    out_shape = [jax.ShapeDtypeStruct((n, RWKV_COLS), BF16), jax.ShapeDtypeStruct((n, 3 * ATT_W), F32)]
    if tail_tiles:
        skip = pos_tiles - tail_tiles
        tail = lambda i: (i // pos_tiles, jnp.maximum(i % pos_tiles - skip, 0), 0)
        out_specs += [pl.BlockSpec((None, tm, ATT_W), tail)] * 2
        out_shape += [jax.ShapeDtypeStruct((n // (pos_tiles * tm), tail_tiles * tm, ATT_W), F32)] * 2
    table = _const_spec((pos_tiles * tm, LANES))
    return pl.pallas_call(
        functools.partial(_in_proj_kernel, pos_tiles=pos_tiles),
        grid=(n // tm,),
        in_specs=[pl.BlockSpec((tm, D_MODEL), row), _const_spec((1, D_MODEL)),
                  _const_spec((D_MODEL, D_IN)), table, table],
        out_specs=out_specs,
        out_shape=out_shape,
        compiler_params=_params(("arbitrary",)),
        name="in_proj",
    )(x2d, g, w_bf, cos_t, sin_t)


def _shift_state_kernel(x_ref, g_ref, h_ref, w_ref, hl_ref, ps_ref):
    hl_ref[...] = _rms(x_ref[...], g_ref[...])
    ps_ref[...] = _mm(h_ref[...].astype(BF16), w_ref[...])


def _shift_state(x_last, g, h_prev, w_rw_bf):
    b = x_last.shape[0]
    return pl.pallas_call(
        _shift_state_kernel,
        out_shape=[jax.ShapeDtypeStruct((b, D_MODEL), F32), jax.ShapeDtypeStruct((b, RWKV_COLS), F32)],
        compiler_params=_params(None),
        name="shift_state",
    )(x_last, g, h_prev, w_rw_bf)


def _rwkv_kernel(p_ref, ps_ref, s0_ref, mu_ref, w0_ref, a0_ref, lora_hi_ref, lora_lo_ref,
                 wg_hi_ref, wg_lo_ref, kk_ref, ka_ref, rk_ref, lnw_ref, lnb_ref,
                 y_ref, sfin_ref, s_scr, prev_scr, pre_scr, bonus_scr, *, C, n_chunks, nb):
    ci = pl.program_id(1)
    C2 = 2 * C
    rows = nb * C
    seqs = range(nb)
    rsl = [slice(C * s, C * (s + 1)) for s in seqs]

    @pl.when(ci == 0)
    def _():
        prev_scr[...] = ps_ref[...]
        z = jnp.zeros((HEAD_DIM, HEAD_DIM), F32)
        for s in seqs:
            for p in range(N_PAIRS):
                top = jnp.concatenate([s0_ref[s, 2 * p], z], axis=1)
                bot = jnp.concatenate([z, s0_ref[s, 2 * p + 1]], axis=1)
                s_scr[s * N_PAIRS + p] = jnp.concatenate([top, bot], axis=0)

    lane = lax.broadcasted_iota(jnp.int32, (1, LANES), 1)
    lane_lo = lane < HEAD_DIM
    ri = lax.broadcasted_iota(jnp.int32, (LANES, LANES), 0)
    cj = lax.broadcasted_iota(jnp.int32, (LANES, LANES), 1)
    seg_ones = jnp.where((ri < HEAD_DIM) == (cj < HEAD_DIM), 1.0, 0.0).astype(BF16)
    row_c = lax.broadcasted_iota(jnp.int32, (C, 1), 0)
    tri = jnp.where(lax.broadcasted_iota(jnp.int32, (C, C), 0) >= lax.broadcasted_iota(jnp.int32, (C, C), 1),
                    1.0, 0.0).astype(BF16)
    mid = C // 2 - 1

    def mixed_all(col0, width):
        cols = slice(col0, col0 + width)
        cur = p_ref[:, :, cols].astype(F32).reshape(rows, width)
        prev = jnp.concatenate([jnp.broadcast_to(prev_scr[s, :, cols], (C, width)) for s in seqs], axis=0)
        row_all = lax.broadcasted_iota(jnp.int32, (rows, 1), 0) % C
        shifted = jnp.where(row_all == 0, prev, pltpu.roll(cur, 1, 0))
        for s in seqs:
            prev_scr[s, :, cols] = cur[C * (s + 1) - 1:C * (s + 1), :]
        return cur + (shifted - cur) * mu_ref[:, cols]

    xl = mixed_all(3 * RWKV_W, LANES)
    gl = mixed_all(3 * RWKV_W + LANES, D_GATE)
    th, tl = _split(jnp.where(lane_lo, jnp.tanh(xl), xl))
    pre_scr[:, PRE_W:PRE_W + 2 * RWKV_W] = (_mm(th, lora_hi_ref[...])
                                            + (_mm(th, lora_lo_ref[...]) + _mm(tl, lora_hi_ref[...])))
    sgh, sgl = _split(_sigmoid_tanh(gl))
    pre_scr[:, PRE_GATE:PRE_GATE + RWKV_W] = (_mm(sgh, wg_hi_ref[...])
                                              + (_mm(sgh, wg_lo_ref[...]) + _mm(sgl, wg_hi_ref[...])))
    k_all = mixed_all(RWKV_W, RWKV_W)
    pre_scr[:, PRE_K:PRE_K + RWKV_W] = k_all
    kkr = k_all * kk_ref[...]
    for p in range(N_PAIRS):
        sl = slice(LANES * p, LANES * (p + 1))
        pre_scr[:, PRE_SQ + LANES * p:PRE_SQ + LANES * (p + 1)] = _mm_split_lhs((kkr * kkr)[:, sl], seg_ones)

    def stack(x):
        return jnp.concatenate([jnp.where(lane_lo, x, 0.0), jnp.where(lane_lo, 0.0, x)], axis=0)

    def unit_inputs(s, p):
        def mixed(col0):
            cols = slice(col0, col0 + LANES)
            cur = p_ref[s, :, cols].astype(F32)
            shifted = jnp.where(row_c == 0, prev_scr[s, :, cols], pltpu.roll(cur, 1, 0))
            prev_scr[s, :, cols] = cur[C - 1:C, :]
            return cur + (shifted - cur) * mu_ref[:, cols]

        sl = slice(LANES * p, LANES * (p + 1))
        rs = rsl[s]
        pre = lambda base: pre_scr[rs, base + LANES * p:base + LANES * (p + 1)]
        r = mixed(LANES * p)
        v = mixed(2 * RWKV_W + LANES * p)
        k = pre(PRE_K)
        logd = -EXP_M_HALF * _sigmoid(w0_ref[:, sl] + pre(PRE_W))
        a_lr = _sigmoid_tanh(a0_ref[:, sl] + pre(PRE_W + RWKV_W))
        kkn = k * kk_ref[:, sl] / jnp.maximum(jnp.sqrt(pre(PRE_SQ)), 1e-12)
        kmod = k * (1.0 + (a_lr - 1.0) * ka_ref[:, sl])
        c = _mm_split_lhs_rhs(tri, logd)
        c_mid = c[mid:mid + 1, :]
        c_end = c[C - 1:C, :]
        cp = c - c_mid
        e_pos = jnp.exp(cp)
        e_neg = jnp.exp(-cp)
        e_prev = jnp.where(row_c == 0, jnp.exp(-c_mid), pltpu.roll(e_pos, 1, 0))
        bonus_scr[rs, sl] = _mm1(r * kmod * rk_ref[:, sl], seg_ones) * v
        return dict(ar=jnp.concatenate([stack(-kkn * e_prev), stack(r * e_pos)], axis=0),
                    bk=jnp.concatenate([stack(kkn * a_lr * e_neg), stack(kmod * e_neg)], axis=0),
                    v=stack(v), e_mid=jnp.exp(c_mid), e_end=jnp.exp(c_end), e_endp=jnp.exp(c_end - c_mid))

    ri2 = lax.broadcasted_iota(jnp.int32, (C2, C2), 0)
    cj2 = lax.broadcasted_iota(jnp.int32, (C2, C2), 1)
    strict = cj2 < ri2
    incl = cj2 <= ri2
    n_sq = C.bit_length() - 2
    fused = C2 % LANES == 0
    pairs = range(nb * N_PAIRS)
    sls = [slice(LANES * p, LANES * (p + 1)) for _ in seqs for p in range(N_PAIRS)]
    units = [unit_inputs(s, p) for s in seqs for p in range(N_PAIRS)]
    ar = [un["ar"] for un in units]
    bk = [un["bk"] for un in units]
    v_st = [un["v"] for un in units]
    e_mid = [un["e_mid"] for un in units]
    e_end = [un["e_end"] for un in units]
    e_endp = [un["e_endp"] for un in units]
    if fused:
        tri_mask = jnp.concatenate([jnp.concatenate([strict, strict], axis=1),
                                    jnp.concatenate([incl, incl], axis=1)], axis=0)
        gram = [jnp.where(tri_mask, _mm_gram(ar[p], bk[p], NT), 0.0) for p in pairs]
        g_ab = [g[:C2, :C2] for g in gram]
        g_ak = [g[:C2, C2:] for g in gram]
        g_rb = [g[C2:, :C2] for g in gram]
        g_rk = [g[C2:, C2:] for g in gram]
    else:
        g_ab = [jnp.where(strict, _mm_gram(ar[p][:C2], bk[p][:C2], NT), 0.0) for p in pairs]
        g_ak = [jnp.where(strict, _mm_gram(ar[p][:C2], bk[p][C2:], NT), 0.0) for p in pairs]
        g_rb = [jnp.where(incl, _mm_gram(ar[p][C2:], bk[p][:C2], NT), 0.0) for p in pairs]
        g_rk = [jnp.where(incl, _mm_gram(ar[p][C2:], bk[p][C2:], NT), 0.0) for p in pairs]

    s_old = [s_scr[p] for p in pairs]
    ws = [_mm_state(ar[p] * e_mid[p], s_old[p], NT) for p in pairs]
    u = [ws[p][:C2] + _mm_apply(g_ak[p], v_st[p]) for p in pairs]
    lp = g_ab
    u = [u[p] + _mm_apply(lp[p], u[p]) for p in pairs]
    for _ in range(n_sq):
        lp = [_mm_inv(lp[p], lp[p]) for p in pairs]
        u = [u[p] + _mm_apply(lp[p], u[p]) for p in pairs]
    uv = [jnp.concatenate([u[p], v_st[p]], axis=0) for p in pairs]
    if fused:
        y_st = [ws[p][C2:] + _mm_state(jnp.concatenate([g_rb[p], g_rk[p]], axis=1), uv[p]) for p in pairs]
    else:
        y_st = [ws[p][C2:] + _mm_state(g_rb[p], u[p]) + _mm_state(g_rk[p], v_st[p]) for p in pairs]
    for p in pairs:
        s_scr[p] = s_old[p] * e_end[p] + _mm_state(uv[p], bk[p] * e_endp[p], TN)

    y = [jnp.concatenate([y_st[s * N_PAIRS + p][:C] + y_st[s * N_PAIRS + p][C:] for s in seqs], axis=0)
         for p in range(N_PAIRS)]
    for p in range(N_PAIRS):
        sl = sls[p]
        dev = y[p] - _mm1(y[p], seg_ones) * (1.0 / HEAD_DIM)
        var = _mm1(dev * dev, seg_ones) * (1.0 / HEAD_DIM)
        yn = dev * lax.rsqrt(var + LNX_EPS) * lnw_ref[:, sl] + lnb_ref[:, sl]
        gate = pre_scr[:, PRE_GATE + LANES * p:PRE_GATE + LANES * (p + 1)]
        y_ref[:, :, sl] = ((yn + bonus_scr[:, sl]) * gate).reshape(nb, C, LANES).astype(y_ref.dtype)

    @pl.when(ci == n_chunks - 1)
    def _():
        for s in seqs:
            for p in range(N_PAIRS):
                s_fin = s_scr[s * N_PAIRS + p]
                sfin_ref[s, 2 * p] = s_fin[:HEAD_DIM, :HEAD_DIM]
                sfin_ref[s, 2 * p + 1] = s_fin[HEAD_DIM:, HEAD_DIM:]


RWKV_SEQS = 8
PRE_W = 0
PRE_GATE = 2 * RWKV_W
PRE_K = 3 * RWKV_W
PRE_SQ = 4 * RWKV_W
PRE_COLS = 5 * RWKV_W


def _rwkv(proj_rw, pstart, s0, wts, C):
    b, t, _ = proj_rw.shape
    nb = RWKV_SEQS
    assert b % nb == 0 and t % C == 0
    n_chunks = t // C
    vec = lambda n: _const_spec((1, n))
    kern = functools.partial(_rwkv_kernel, C=C, n_chunks=n_chunks, nb=nb)
    return pl.pallas_call(
        kern,
        grid=(b // nb, n_chunks),
        in_specs=[pl.BlockSpec((nb, C, RWKV_COLS), lambda i, j: (i, j, 0)),
                  pl.BlockSpec((nb, 1, RWKV_COLS), lambda i, j: (i, 0, 0)),
                  pl.BlockSpec((nb, N_HEADS, HEAD_DIM, HEAD_DIM), lambda i, j: (i, 0, 0, 0)),
                  vec(RWKV_COLS), vec(RWKV_W), vec(RWKV_W),
                  _const_spec((LANES, 2 * RWKV_W)), _const_spec((LANES, 2 * RWKV_W)),
                  _const_spec((D_GATE, RWKV_W)), _const_spec((D_GATE, RWKV_W)),
                  vec(RWKV_W), vec(RWKV_W), vec(RWKV_W), vec(RWKV_W), vec(RWKV_W)],
        out_specs=[pl.BlockSpec((nb, C, RWKV_W), lambda i, j: (i, j, 0)),
                   pl.BlockSpec((nb, N_HEADS, HEAD_DIM, HEAD_DIM), lambda i, j: (i, 0, 0, 0))],
        out_shape=[jax.ShapeDtypeStruct((b, t, RWKV_W), BF16 if C % (2 * SUBLANES) == 0 else F32),
                   jax.ShapeDtypeStruct((b, N_HEADS, HEAD_DIM, HEAD_DIM), F32)],
        scratch_shapes=[pltpu.VMEM((nb * N_PAIRS, LANES, LANES), F32), pltpu.VMEM((nb, 1, RWKV_COLS), F32),
                        pltpu.VMEM((nb * C, PRE_COLS), F32), pltpu.VMEM((nb * C, RWKV_W), F32)],
        compiler_params=_params(("parallel", "arbitrary")),
        name="rwkv",
    )(proj_rw, pstart, s0, *wts)


ATT_UNROLL = 8
LOG2E = 1.4426950408889634


def _attn_prompt_kernel(q_ref, k_ref, v_ref, bias_ref, o_ref, acc_scr, m_scr, l_scr, *, T):
    QB = N_STEPS
    lane = lax.broadcasted_iota(jnp.int32, (1, LANES), 1)
    lane_lo = lane < HEAD_DIM

    def halves(x):
        return jnp.where(lane_lo, x[:QB], x[QB:])

    order = sorted(DILATIONS, reverse=True)
    assert order[-1] == 1
    for bi, d in enumerate(order):
        per_res = T // (QB * d)
        ub = min(ATT_UNROLL, per_res)
        groups = ATT_UNROLL // ub
        gpr = per_res // ub

        def rows(start, d=d):
            if d == 1:
                return pl.ds(pl.multiple_of(start, QB), QB)
            return pl.ds(start, QB, stride=d)

        def body(it, carry, bi=bi, d=d, ub=ub, groups=groups, gpr=gpr, rows=rows):
            blocks = []
            for g in range(groups):
                gi = it * groups + g
                res = gi // gpr
                n0 = (gi % gpr) * ub
                starts = [res + d * QB * jnp.maximum(n0 - 1, 0)] + [res + d * QB * (n0 + u) for u in range(ub)]
                kb = [k_ref[rows(s), :].astype(BF16) for s in starts]
                vb = [v_ref[rows(s), :].astype(BF16) for s in starts]
                for u in range(ub):
                    bias = bias_ref[jnp.where(n0 == 0, 1, 0)] if u == 0 else bias_ref[0]
                    blocks.append((starts[u + 1], jnp.concatenate([kb[u], kb[u + 1]], axis=0),
                                   jnp.concatenate([vb[u], vb[u + 1]], axis=0), bias))
            q_st = []
            for start, _, _, _ in blocks:
                q = q_ref[rows(start), :] * (HEAD_DIM ** -0.5 * LOG2E)
                q_st.append(jnp.concatenate([jnp.where(lane_lo, q, 0.0), jnp.where(lane_lo, 0.0, q)],
                                            axis=0).astype(BF16))
            s = [_mm(q_st[i], blk[1], NT) + blk[3] for i, blk in enumerate(blocks)]
            m = [jnp.max(x, axis=1, keepdims=True) for x in s]
            p = [jnp.exp2(s[i] - m[i]) for i in range(len(blocks))]
            l = [jnp.sum(x, axis=1, keepdims=True) for x in p]
            o = [halves(_mm(p[i].astype(BF16), blk[2])) for i, blk in enumerate(blocks)]
            m_b = [halves(jnp.broadcast_to(x, (2 * QB, LANES))) for x in m]
            l_b = [halves(jnp.broadcast_to(x, (2 * QB, LANES))) for x in l]
            if bi == 0:
                for i, blk in enumerate(blocks):
                    acc_scr[rows(blk[0]), :] = o[i]
                    m_scr[rows(blk[0]), :] = m_b[i]
                    l_scr[rows(blk[0]), :] = l_b[i]
            else:
                m_old = [m_scr[rows(blk[0]), :] for blk in blocks]
                acc_old = [acc_scr[rows(blk[0]), :] for blk in blocks]
                l_old = [l_scr[rows(blk[0]), :] for blk in blocks]
                for i, blk in enumerate(blocks):
                    m_new = jnp.maximum(m_old[i], m_b[i])
                    w_old = jnp.exp2(m_old[i] - m_new)
                    w_new = jnp.exp2(m_b[i] - m_new)
                    acc = acc_old[i] * w_old + o[i] * w_new
                    den = l_old[i] * w_old + l_b[i] * w_new
                    if bi < len(order) - 1:
                        acc_scr[rows(blk[0]), :] = acc
                        m_scr[rows(blk[0]), :] = m_new
                        l_scr[rows(blk[0]), :] = den
                    else:
                        o_ref[rows(blk[0]), :] = (acc / den).astype(o_ref.dtype)
            return carry

        lax.fori_loop(0, d * gpr // groups, body, 0)


def _band_bias():
    qi = jnp.arange(2 * N_STEPS)[:, None] % N_STEPS
    kj = jnp.arange(2 * N_STEPS)[None, :]
    band = (kj >= qi) & (kj <= qi + N_STEPS)
    normal = jnp.where(band, 0.0, NEG_INF).astype(F32)
    first = jnp.where(band & (kj >= N_STEPS), 0.0, NEG_INF).astype(F32)
    return jnp.stack([normal, first])


def _attn_prompt(qkv):
    b, t, _ = qkv.shape
    assert t % (N_STEPS * max(DILATIONS)) == 0
    spec = pl.BlockSpec((None, t, LANES), lambda i, p: (i, 0, p))
    slab = lambda j: pl.BlockSpec((None, t, LANES), lambda i, p, j=j: (i, 0, j * N_PAIRS + p))
    q, k, v = qkv, qkv, qkv
    return pl.pallas_call(
        functools.partial(_attn_prompt_kernel, T=t),
        grid=(b, N_PAIRS),
        in_specs=[slab(0), slab(1), slab(2), _const_spec((2, 2 * N_STEPS, 2 * N_STEPS))],
        out_specs=spec,
        out_shape=jax.ShapeDtypeStruct((b, t, ATT_W), BF16),
        scratch_shapes=[pltpu.VMEM((t, LANES), F32)] * 3,
        compiler_params=_params(("parallel", "parallel")),
        name="attn_prompt",
    )(q, k, v, _band_bias())


def _attn_sample_kernel(q_ref, kn_ref, vn_ref, kc_ref, vc_ref, cntc_ref, cntn_ref,
                        o_ref, ko_ref, vo_ref, *, ts, w_buf):
    rows = N_HEADS * ts
    ri = lax.broadcasted_iota(jnp.int32, (rows, ATT_W), 0)
    cj = lax.broadcasted_iota(jnp.int32, (rows, ATT_W), 1)
    own = (ri // ts) == (cj // HEAD_DIM)
    q_all = jnp.concatenate([q_ref[...] * (HEAD_DIM ** -0.5)] * N_HEADS, axis=0)
    q_st = jnp.where(own, q_all, 0.0).astype(BF16)
    kc = kc_ref[...]
    vc = vc_ref[...]
    kn = kn_ref[...]
    vn = vn_ref[...]
    cnt_c = cntc_ref[...]
    cnt_n = cntn_ref[...]
    s_c = jnp.where(cnt_c > 0.0, _mm(q_st, kc.astype(BF16), NT), NEG_INF)
    s_n = jnp.where(cnt_n > 0.0, _mm(q_st, kn.astype(BF16), NT), NEG_INF)
    m = jnp.maximum(jnp.max(s_c, axis=1, keepdims=True), jnp.max(s_n, axis=1, keepdims=True))
    p_c = cnt_c * jnp.exp(s_c - m)
    p_n = cnt_n * jnp.exp(s_n - m)
    den = jnp.sum(p_c, axis=1, keepdims=True) + jnp.sum(p_n, axis=1, keepdims=True)
    o_st = (_mm(p_c.astype(BF16), vc.astype(BF16)) + _mm(p_n.astype(BF16), vn.astype(BF16))) / den
    o_st = jnp.where(own, o_st, 0.0)
    out = o_st[0:ts]
    for h in range(1, N_HEADS):
        out = out + o_st[h * ts:(h + 1) * ts]
    o_ref[...] = out
    ko_ref[0:w_buf - ts, :] = kc[ts:w_buf]
    ko_ref[w_buf - ts:w_buf, :] = kn
    vo_ref[0:w_buf - ts, :] = vc[ts:w_buf]
    vo_ref[w_buf - ts:w_buf, :] = vn


def _branch_counts(ts, w_buf):
    delta = (w_buf + jnp.arange(ts)[:, None]) - jnp.arange(w_buf + ts)[None, :]
    cnt = jnp.zeros(delta.shape, F32)
    for d in DILATIONS:
        cnt = cnt + ((delta >= 0) & (delta % d == 0) & (delta // d <= N_STEPS)).astype(F32)
    cnt = jnp.tile(cnt, (N_HEADS, 1))
    return cnt[:, :w_buf], cnt[:, w_buf:]


def _attn_sample(q, kn, vn, kcache, vcache):
    b, ts, _ = q.shape
    w_buf = kcache.shape[1]
    assert w_buf >= N_STEPS * max(DILATIONS) and ts % SUBLANES == 0
    cnt_c, cnt_n = _branch_counts(ts, w_buf)
    new = pl.BlockSpec((None, ts, ATT_W), lambda i: (i, 0, 0))
    cache = pl.BlockSpec((None, w_buf, ATT_W), lambda i: (i, 0, 0))
    return pl.pallas_call(
        functools.partial(_attn_sample_kernel, ts=ts, w_buf=w_buf),
        grid=(b,),
        in_specs=[new, new, new, cache, cache,
                  _const_spec((N_HEADS * ts, w_buf)), _const_spec((N_HEADS * ts, ts))],
        out_specs=[new, cache, cache],
        out_shape=[jax.ShapeDtypeStruct((b, ts, ATT_W), F32),
                   jax.ShapeDtypeStruct((b, w_buf, ATT_W), F32),
                   jax.ShapeDtypeStruct((b, w_buf, ATT_W), F32)],
        compiler_params=_params(("parallel",)),
        name="attn_sample",
    )(q, kn, vn, kcache, vcache, cnt_c, cnt_n)


FFN_TF = 256
N_FCHUNK = D_FF // FFN_TF


def _mlp_kernel(*refs, tm, tiles_per_seq, seq_len, has_state):
    if has_state:
        (x_ref, rw_ref, att_ref, wout_ref, gmix_ref, gpre_ref, gpost_ref, wup_ref, cw_ref, cb_ref,
         wd_ref, p2_ref, y_ref, u_ref, act_scr) = refs
    else:
        (x_ref, rw_ref, att_ref, wout_ref, gmix_ref, gpre_ref, gpost_ref, wup_ref, cw_ref, cb_ref,
         wd_ref, y_ref, u_ref, act_scr, ubuf_scr, carry_scr) = refs

        @pl.when((pl.program_id(0) % tiles_per_seq) == 0)
        def _():
            carry_scr[...] = jnp.zeros_like(carry_scr)

    mix = (_mm(rw_ref[...].astype(BF16), wout_ref[:RWKV_W, :])
           + _mm(att_ref[...].astype(BF16), wout_ref[RWKV_W:, :]))
    x1 = x_ref[...] + _rms(mix, gmix_ref[...])
    hb = _rms(x1, gpre_ref[...]).astype(BF16)
    t_in = lax.broadcasted_iota(jnp.int32, (tm, 1), 0) % seq_len

    for c in range(N_FCHUNK):
        conv = []
        for idx, col0 in enumerate((c * FFN_TF, D_FF + c * FFN_TF)):
            cols = slice(col0, col0 + FFN_TF)
            u = _mm(hb, wup_ref[:, cols])
            if has_state:
                p2 = p2_ref[:, cols]
                u1 = jnp.where(t_in >= 1, pltpu.roll(u, 1, 0), pltpu.roll(p2, tm - 1, 0))
                u2 = jnp.where(t_in >= 2, pltpu.roll(u, 2, 0), p2)
                u_ref[:, cols] = u
            else:
                stage = ubuf_scr.at[c % 2, idx]
                stage[0:SUBLANES, :] = carry_scr[c, idx]
                stage[SUBLANES:, :] = u
                u1 = stage[SUBLANES - 1:SUBLANES - 1 + tm, :]
                u2 = stage[SUBLANES - 2:SUBLANES - 2 + tm, :]
                carry_scr[c, idx] = u[tm - SUBLANES:]
                u_ref[:, cols] = u[tm - SUBLANES:]
            conv.append(u2 * cw_ref[0:1, cols] + u1 * cw_ref[1:2, cols] + u * cw_ref[2:3, cols]
                        + cb_ref[:, cols])
        act_scr[:, c * FFN_TF:(c + 1) * FFN_TF] = (conv[0] * _sigmoid(conv[0]) * conv[1]).astype(BF16)

    y_ref[...] = x1 + _rms(_mm(act_scr[...], wd_ref[...]), gpost_ref[...])


def _mlp(x2d, rw, att, w, tm, seq_len, state_rows=None):
    n = x2d.shape[0]
    has_state = state_rows is not None
    tiles_per_seq = max(seq_len // tm, 1)
    n_tiles = n // tm
    row = lambda i: (i, 0)
    vec = lambda: _const_spec((1, D_MODEL))
    in_specs = [pl.BlockSpec((tm, D_MODEL), row), pl.BlockSpec((tm, RWKV_W), row),
                pl.BlockSpec((tm, ATT_W), row), _const_spec((D_MODEL, D_MODEL)), vec(), vec(), vec(),
                _const_spec((D_MODEL, 2 * D_FF)), _const_spec((CONV_W, 2 * D_FF)),
                _const_spec((1, 2 * D_FF)), _const_spec((D_FF, D_MODEL))]
    args = [x2d, rw, att, w["w_out"], w["norm_mix_post"], w["norm_ffn_pre"], w["norm_ffn_post"],
            w["w_ffn_up"], w["ffn_conv_w"], w["ffn_conv_b"], w["w_ffn_down"]]
    scratch = [pltpu.VMEM((tm, D_FF), BF16)]
    if has_state:
        assert tm % seq_len == 0 and seq_len == SUBLANES
        in_specs.append(pl.BlockSpec((tm, 2 * D_FF), row))
        args.append(state_rows)
        u_spec = pl.BlockSpec((tm, 2 * D_FF), row)
        u_shape = jax.ShapeDtypeStruct((n, 2 * D_FF), F32)
    else:
        assert seq_len % tm == 0
        scratch += [pltpu.VMEM((2, 2, tm + SUBLANES, FFN_TF), F32),
                    pltpu.VMEM((N_FCHUNK, 2, SUBLANES, FFN_TF), F32)]
        u_spec = pl.BlockSpec((None, SUBLANES, 2 * D_FF), lambda i: (i, 0, 0))
        u_shape = jax.ShapeDtypeStruct((n_tiles, SUBLANES, 2 * D_FF), F32)
    kern = functools.partial(_mlp_kernel, tm=tm, tiles_per_seq=tiles_per_seq, seq_len=seq_len,
                             has_state=has_state)
    return pl.pallas_call(
        kern,
        grid=(n_tiles,),
        in_specs=in_specs,
        out_specs=[pl.BlockSpec((tm, D_MODEL), row), u_spec],
        out_shape=[jax.ShapeDtypeStruct((n, D_MODEL), F32), u_shape],
        scratch_shapes=scratch,
        compiler_params=_params(("arbitrary",)),
        name="mlp",
    )(*args)


def _rope_tables(pos):
    half = HEAD_DIM // 2
    inv = ROPE_THETA ** (-jnp.arange(half, dtype=F32) / half)
    ang = pos.astype(F32)[:, None] * inv[None, :]
    cos, sin = jnp.cos(ang), jnp.sin(ang)
    cos_t = jnp.tile(jnp.concatenate([cos, cos], axis=1), (1, LANES // HEAD_DIM))
    sin_t = jnp.tile(jnp.concatenate([-sin, sin], axis=1), (1, LANES // HEAD_DIM))
    return cos_t, sin_t


def _layer(x, pos, h_prev, wkv0, k_buf, v_buf, conv_prev, w, tm, chunk, mlp_tm):
    b, t, _ = x.shape
    n = b * t
    x2d = x.reshape(n, D_MODEL)
    cos_t, sin_t = _rope_tables(pos)
    if t < tm:
        cos_t = jnp.tile(cos_t, (tm // t, 1))
        sin_t = jnp.tile(sin_t, (tm // t, 1))
        pos_tiles = 1
    else:
        pos_tiles = t // tm
    keep = min(N_STEPS * max(DILATIONS), t)
    tail_tiles = keep // tm if k_buf is None else 0
    rw, qkv, *tails = _in_proj(x2d, w["norm_mix_pre"], w["w_in"], cos_t, sin_t, tm, pos_tiles, tail_tiles)
    h_last, pstart = _shift_state(x[:, -1], w["norm_mix_pre"], h_prev, w["w_in_rw"])

    rwkv_out, wkv_new = _rwkv(rw.reshape(b, t, RWKV_COLS), pstart.reshape(b, 1, RWKV_COLS), wkv0,
                              w["rwkv"], chunk)
    qkv3 = qkv.reshape(b, t, 3 * ATT_W)
    if k_buf is None:
        att = _attn_prompt(qkv3)
        k_new, v_new = tails
    else:
        w_buf = k_buf.shape[1]
        q3, k3, v3 = (qkv3[:, :, ATT_W * i:ATT_W * (i + 1)] for i in range(3))
        att, k_new, v_new = _attn_sample(q3, k3, v3, k_buf.reshape(b, w_buf, ATT_W),
                                         v_buf.reshape(b, w_buf, ATT_W))
    rw2d, att2d = rwkv_out.reshape(n, RWKV_W), att.reshape(n, ATT_W)
    if conv_prev is None:
        y, tails = _mlp(x2d, rw2d, att2d, w, mlp_tm, t)
        conv_new = tails.reshape(b, t // mlp_tm, SUBLANES, 2 * D_FF)[:, -1, SUBLANES - (CONV_W - 1):]
    else:
        p2 = jnp.pad(conv_prev, ((0, 0), (0, t - (CONV_W - 1)), (0, 0))).reshape(n, 2 * D_FF)
        y, u = _mlp(x2d, rw2d, att2d, w, mlp_tm, t, state_rows=p2)
        conv_new = u.reshape(b, t, 2 * D_FF)[:, t - (CONV_W - 1):]
    heads = lambda a: a.reshape(b, a.shape[1], N_HEADS, HEAD_DIM)
    return y.reshape(b, t, D_MODEL), h_last, wkv_new, heads(k_new), heads(v_new), conv_new


def _prep_weights(norm_mix_pre, norm_mix_post, norm_ffn_pre, norm_ffn_post, w_in, mu_shift, w0,
                  w_decay_up, a0, w_iclr_up, w_gate_up, k_k, k_a, r_k, lnx_w, lnx_b, w_out,
                  w_ffn_up, ffn_conv_w, ffn_conv_b, w_ffn_down):
    vec = lambda a: a.reshape(1, -1)
    zero = jnp.zeros((D_DECAY, RWKV_W), F32)
    lora = jnp.concatenate([jnp.concatenate([w_decay_up, zero], axis=1),
                            jnp.concatenate([zero, w_iclr_up], axis=1)], axis=0)
    lora_hi, lora_lo = _split(lora)
    wg_hi, wg_lo = _split(w_gate_up)
    w_in_bf = w_in.astype(BF16)
    return {
        "norm_mix_pre": vec(norm_mix_pre), "norm_mix_post": vec(norm_mix_post),
        "norm_ffn_pre": vec(norm_ffn_pre), "norm_ffn_post": vec(norm_ffn_post),
        "w_in": w_in_bf, "w_in_rw": w_in_bf[:, :RWKV_COLS],
        "rwkv": (vec(mu_shift), vec(w0), vec(a0), lora_hi, lora_lo, wg_hi, wg_lo,
                 vec(k_k), vec(k_a), vec(r_k), vec(lnx_w), vec(lnx_b)),
        "w_out": w_out.astype(BF16), "w_ffn_up": w_ffn_up.astype(BF16),
        "ffn_conv_w": ffn_conv_w, "ffn_conv_b": vec(ffn_conv_b), "w_ffn_down": w_ffn_down.astype(BF16),
    }


PROMPT_TM = 512
PROMPT_CHUNK = 64
SAMPLE_MLP_TM = 128


def kernel(x_prompt, x_sample, state_rwkv_shift, state_rwkv_wkv, cache_att_k, cache_att_v, state_ffn_conv, norm_mix_pre, norm_mix_post, norm_ffn_pre, norm_ffn_post, w_in, mu_shift, w0, w_decay_up, a0, w_iclr_up, w_gate_up, k_k, k_a, r_k, lnx_w, lnx_b, w_out, w_ffn_up, ffn_conv_w, ffn_conv_b, w_ffn_down):
    bp, tp, _ = x_prompt.shape
    bs, ts, _ = x_sample.shape
    depth = norm_mix_pre.shape[0]
    pos_p = jnp.arange(tp, dtype=jnp.int32)
    pos_s = PAST_LEN + jnp.arange(ts, dtype=jnp.int32)
    yp, ys = x_prompt, x_sample
    outs_p = [[] for _ in range(5)]
    outs_s = [[] for _ in range(5)]
    for l in range(depth):
        w = _prep_weights(norm_mix_pre[l], norm_mix_post[l], norm_ffn_pre[l], norm_ffn_post[l], w_in[l],
                          mu_shift[l], w0[l], w_decay_up[l], a0[l], w_iclr_up[l], w_gate_up[l], k_k[l],
                          k_a[l], r_k[l], lnx_w[l], lnx_b[l], w_out[l], w_ffn_up[l], ffn_conv_w[l],
                          ffn_conv_b[l], w_ffn_down[l])
        yp, *state_p = _layer(yp, pos_p, jnp.zeros((bp, D_MODEL), F32),
                              jnp.zeros((bp, N_HEADS, HEAD_DIM, HEAD_DIM), F32), None, None, None,
                              w, min(PROMPT_TM, tp), min(PROMPT_CHUNK, tp), min(PROMPT_TM, tp))
        ys, *state_s = _layer(ys, pos_s, state_rwkv_shift[l], state_rwkv_wkv[l], cache_att_k[l],
                              cache_att_v[l], state_ffn_conv[l], w, bs * ts, ts, SAMPLE_MLP_TM)
        for acc, val in zip(outs_p, state_p):
            acc.append(val)
        for acc, val in zip(outs_s, state_s):
            acc.append(val)
    return (yp, ys, *(jnp.stack(a) for a in outs_p), *(jnp.stack(a) for a in outs_s))
```

```python
import functools

import jax
import jax.numpy as jnp
from jax import lax
from jax.experimental import pallas as pl
from jax.experimental.pallas import tpu as pltpu

F32 = jnp.float32
BF16 = jnp.bfloat16

D_MODEL = 1024
HEAD_DIM = 64
RWKV_W = 512
ATT_W = 512
N_HEADS = 8
N_PAIRS = N_HEADS // 2
D_DECAY = 64
D_ICLR = 64
D_GATE = 128
RWKV_COLS = 3 * RWKV_W + D_DECAY + D_ICLR + D_GATE
D_IN = RWKV_COLS + 3 * ATT_W
DILATIONS = (1, 4, 16)
N_STEPS = 128
ROPE_THETA = 10000.0
D_FF = 2816
CONV_W = 3
NORM_EPS = 1e-6
LNX_EPS = 64e-5
NEG_INF = -1e30
PAST_LEN = 16384

LANES = 128
SUBLANES = 8
VMEM_LIMIT = 56 * 1024 * 1024

NN = (((1,), (0,)), ((), ()))
NT = (((1,), (1,)), ((), ()))
TN = (((0,), (0,)), ((), ()))


def _mm(a, b, dims=NN):
    return lax.dot_general(a, b, dims, preferred_element_type=F32)


def _split(x):
    hi = x.astype(BF16)
    lo = (x - hi.astype(F32)).astype(BF16)
    return hi, lo


def _mm1(a, b, dims=NN):
    return _mm(a.astype(BF16), b.astype(BF16), dims)


_mm_gram = _mm1
_mm_inv = _mm1
_mm_apply = _mm1
_mm_state = _mm1


def _mm_split_lhs(a, b_bf16):
    ah, al = _split(a)
    return _mm(ah, b_bf16) + _mm(al, b_bf16)


def _mm_split_lhs_rhs(a_bf16, b):
    bh, bl = _split(b)
    return _mm(a_bf16, bh) + _mm(a_bf16, bl)


def _rms(x, g):
    return x * lax.rsqrt(jnp.mean(x * x, axis=-1, keepdims=True) + NORM_EPS) * g


def _sigmoid(x):
    return 1.0 / (1.0 + jnp.exp(-x))


def _sigmoid_tanh(x):
    return 0.5 + 0.5 * jnp.tanh(0.5 * x)


EXP_M_HALF = 0.6065306597126334


def _params(sem):
    return pltpu.CompilerParams(dimension_semantics=sem, vmem_limit_bytes=VMEM_LIMIT)


def _const_spec(shape):
    nd = len(shape)
    return pl.BlockSpec(shape, lambda *_: (0,) * nd, pipeline_mode=pl.Buffered(1))


def _in_proj_kernel(x_ref, g_ref, w_ref, cos_ref, sin_ref, rw_ref, qkv_ref, *tail_refs, pos_tiles):
    lane = lax.broadcasted_iota(jnp.int32, (1, ATT_W), 1)
    first_half = (lane % HEAD_DIM) < HEAD_DIM // 2
    q0 = RWKV_COLS
    tm = x_ref.shape[0]
    hb = _rms(x_ref[...], g_ref[...]).astype(BF16)
    rw_ref[...] = _mm(hb, w_ref[:, :RWKV_COLS]).astype(BF16)
    pos_rows = pl.ds(pl.multiple_of((pl.program_id(0) % pos_tiles) * tm, tm), tm)
    cos = jnp.concatenate([cos_ref[pos_rows, :]] * (ATT_W // LANES), axis=1)
    sin = jnp.concatenate([sin_ref[pos_rows, :]] * (ATT_W // LANES), axis=1)

    def rope(t):
        partner = jnp.where(first_half, pltpu.roll(t, ATT_W - HEAD_DIM // 2, 1),
                            pltpu.roll(t, HEAD_DIM // 2, 1))
        return t * cos + partner * sin

    qkv_ref[:, 0:ATT_W] = rope(_mm(hb, w_ref[:, q0:q0 + ATT_W]))
    k = rope(_mm(hb, w_ref[:, q0 + ATT_W:q0 + 2 * ATT_W]))
    v = _mm(hb, w_ref[:, q0 + 2 * ATT_W:q0 + 3 * ATT_W])
    qkv_ref[:, ATT_W:2 * ATT_W] = k
    qkv_ref[:, 2 * ATT_W:3 * ATT_W] = v
    if tail_refs:
        tail_refs[0][...] = k
        tail_refs[1][...] = v


def _in_proj(x2d, g, w_bf, cos_t, sin_t, tm, pos_tiles, tail_tiles=0):
    n = x2d.shape[0]
    row = lambda i: (i, 0)
    out_specs = [pl.BlockSpec((tm, RWKV_COLS), row), pl.BlockSpec((tm, 3 * ATT_W), row)]
    out_shape = [jax.ShapeDtypeStruct((n, RWKV_COLS), BF16), jax.ShapeDtypeStruct((n, 3 * ATT_W), F32)]
    if tail_tiles:
        skip = pos_tiles - tail_tiles
        tail = lambda i: (i // pos_tiles, jnp.maximum(i % pos_tiles - skip, 0), 0)
        out_specs += [pl.BlockSpec((None, tm, ATT_W), tail)] * 2
        out_shape += [jax.ShapeDtypeStruct((n // (pos_tiles * tm), tail_tiles * tm, ATT_W), F32)] * 2
    table = _const_spec((pos_tiles * tm, LANES))
    return pl.pallas_call(
        functools.partial(_in_proj_kernel, pos_tiles=pos_tiles),
        grid=(n // tm,),
        in_specs=[pl.BlockSpec((tm, D_MODEL), row), _const_spec((1, D_MODEL)),
                  _const_spec((D_MODEL, D_IN)), table, table],
        out_specs=out_specs,
        out_shape=out_shape,
        compiler_params=_params(("arbitrary",)),
        name="in_proj",
    )(x2d, g, w_bf, cos_t, sin_t)


def _shift_state_kernel(x_ref, g_ref, h_ref, w_ref, hl_ref, ps_ref):
    hl_ref[...] = _rms(x_ref[...], g_ref[...])
    ps_ref[...] = _mm(h_ref[...].astype(BF16), w_ref[...])


def _shift_state(x_last, g, h_prev, w_rw_bf):
    b = x_last.shape[0]
    return pl.pallas_call(
        _shift_state_kernel,
        out_shape=[jax.ShapeDtypeStruct((b, D_MODEL), F32), jax.ShapeDtypeStruct((b, RWKV_COLS), F32)],
        compiler_params=_params(None),
        name="shift_state",
    )(x_last, g, h_prev, w_rw_bf)


def _rwkv_kernel(p_ref, ps_ref, s0_ref, mu_ref, w0_ref, a0_ref, lora_hi_ref, lora_lo_ref,
                 wg_hi_ref, wg_lo_ref, kk_ref, ka_ref, rk_ref, lnw_ref, lnb_ref,
                 y_ref, sfin_ref, s_scr, prev_scr, pre_scr, bonus_scr, *, C, n_chunks, nb):
    ci = pl.program_id(1)
    C2 = 2 * C
    rows = nb * C
    seqs = range(nb)
    rsl = [slice(C * s, C * (s + 1)) for s in seqs]

    @pl.when(ci == 0)
    def _():
        prev_scr[...] = ps_ref[...]
        z = jnp.zeros((HEAD_DIM, HEAD_DIM), F32)
        for s in seqs:
            for p in range(N_PAIRS):
                top = jnp.concatenate([s0_ref[s, 2 * p], z], axis=1)
                bot = jnp.concatenate([z, s0_ref[s, 2 * p + 1]], axis=1)
                s_scr[s * N_PAIRS + p] = jnp.concatenate([top, bot], axis=0)

    lane = lax.broadcasted_iota(jnp.int32, (1, LANES), 1)
    lane_lo = lane < HEAD_DIM
    ri = lax.broadcasted_iota(jnp.int32, (LANES, LANES), 0)
    cj = lax.broadcasted_iota(jnp.int32, (LANES, LANES), 1)
    seg_ones = jnp.where((ri < HEAD_DIM) == (cj < HEAD_DIM), 1.0, 0.0).astype(BF16)
    row_c = lax.broadcasted_iota(jnp.int32, (C, 1), 0)
    tri = jnp.where(lax.broadcasted_iota(jnp.int32, (C, C), 0) >= lax.broadcasted_iota(jnp.int32, (C, C), 1),
                    1.0, 0.0).astype(BF16)
    mid = C // 2 - 1

    def mixed_all(col0, width):
        cols = slice(col0, col0 + width)
        cur = p_ref[:, :, cols].astype(F32).reshape(rows, width)
        prev = jnp.concatenate([jnp.broadcast_to(prev_scr[s, :, cols], (C, width)) for s in seqs], axis=0)
        row_all = lax.broadcasted_iota(jnp.int32, (rows, 1), 0) % C
        shifted = jnp.where(row_all == 0, prev, pltpu.roll(cur, 1, 0))
        for s in seqs:
            prev_scr[s, :, cols] = cur[C * (s + 1) - 1:C * (s + 1), :]
        return cur + (shifted - cur) * mu_ref[:, cols]

    xl = mixed_all(3 * RWKV_W, LANES)
    gl = mixed_all(3 * RWKV_W + LANES, D_GATE)
    th, tl = _split(jnp.where(lane_lo, jnp.tanh(xl), xl))
    pre_scr[:, PRE_W:PRE_W + 2 * RWKV_W] = (_mm(th, lora_hi_ref[...])
                                            + (_mm(th, lora_lo_ref[...]) + _mm(tl, lora_hi_ref[...])))
    sgh, sgl = _split(_sigmoid_tanh(gl))
    pre_scr[:, PRE_GATE:PRE_GATE + RWKV_W] = (_mm(sgh, wg_hi_ref[...])
                                              + (_mm(sgh, wg_lo_ref[...]) + _mm(sgl, wg_hi_ref[...])))
    k_all = mixed_all(RWKV_W, RWKV_W)
    pre_scr[:, PRE_K:PRE_K + RWKV_W] = k_all
    kkr = k_all * kk_ref[...]
    for p in range(N_PAIRS):
        sl = slice(LANES * p, LANES * (p + 1))
        pre_scr[:, PRE_SQ + LANES * p:PRE_SQ + LANES * (p + 1)] = _mm_split_lhs((kkr * kkr)[:, sl], seg_ones)

    def stack(x):
        return jnp.concatenate([jnp.where(lane_lo, x, 0.0), jnp.where(lane_lo, 0.0, x)], axis=0)

    def unit_inputs(s, p):
        def mixed(col0):
            cols = slice(col0, col0 + LANES)
            cur = p_ref[s, :, cols].astype(F32)
            shifted = jnp.where(row_c == 0, prev_scr[s, :, cols], pltpu.roll(cur, 1, 0))
            prev_scr[s, :, cols] = cur[C - 1:C, :]
            return cur + (shifted - cur) * mu_ref[:, cols]

        sl = slice(LANES * p, LANES * (p + 1))
        rs = rsl[s]
        pre = lambda base: pre_scr[rs, base + LANES * p:base + LANES * (p + 1)]
        r = mixed(LANES * p)
        v = mixed(2 * RWKV_W + LANES * p)
        k = pre(PRE_K)
        logd = -EXP_M_HALF * _sigmoid(w0_ref[:, sl] + pre(PRE_W))
        a_lr = _sigmoid_tanh(a0_ref[:, sl] + pre(PRE_W + RWKV_W))
        kkn = k * kk_ref[:, sl] / jnp.maximum(jnp.sqrt(pre(PRE_SQ)), 1e-12)
        kmod = k * (1.0 + (a_lr - 1.0) * ka_ref[:, sl])
        c = _mm_split_lhs_rhs(tri, logd)
        c_mid = c[mid:mid + 1, :]
        c_end = c[C - 1:C, :]
        cp = c - c_mid
        e_pos = jnp.exp(cp)
        e_neg = jnp.exp(-cp)
        e_prev = jnp.where(row_c == 0, jnp.exp(-c_mid), pltpu.roll(e_pos, 1, 0))
        bonus_scr[rs, sl] = _mm1(r * kmod * rk_ref[:, sl], seg_ones) * v
        return dict(ar=jnp.concatenate([stack(-kkn * e_prev), stack(r * e_pos)], axis=0),
                    bk=jnp.concatenate([stack(kkn * a_lr * e_neg), stack(kmod * e_neg)], axis=0),
                    v=stack(v), e_mid=jnp.exp(c_mid), e_end=jnp.exp(c_end), e_endp=jnp.exp(c_end - c_mid))

    ri2 = lax.broadcasted_iota(jnp.int32, (C2, C2), 0)
    cj2 = lax.broadcasted_iota(jnp.int32, (C2, C2), 1)
    strict = cj2 < ri2
    incl = cj2 <= ri2
    n_sq = C.bit_length() - 2
    fused = C2 % LANES == 0
    pairs = range(nb * N_PAIRS)
    sls = [slice(LANES * p, LANES * (p + 1)) for _ in seqs for p in range(N_PAIRS)]
    units = [unit_inputs(s, p) for s in seqs for p in range(N_PAIRS)]
    ar = [un["ar"] for un in units]
    bk = [un["bk"] for un in units]
    v_st = [un["v"] for un in units]
    e_mid = [un["e_mid"] for un in units]
    e_end = [un["e_end"] for un in units]
    e_endp = [un["e_endp"] for un in units]
    if fused:
        tri_mask = jnp.concatenate([jnp.concatenate([strict, strict], axis=1),
                                    jnp.concatenate([incl, incl], axis=1)], axis=0)
        gram = [jnp.where(tri_mask, _mm_gram(ar[p], bk[p], NT), 0.0) for p in pairs]
        g_ab = [g[:C2, :C2] for g in gram]
        g_ak = [g[:C2, C2:] for g in gram]
        g_rb = [g[C2:, :C2] for g in gram]
        g_rk = [g[C2:, C2:] for g in gram]
    else:
        g_ab = [jnp.where(strict, _mm_gram(ar[p][:C2], bk[p][:C2], NT), 0.0) for p in pairs]
        g_ak = [jnp.where(strict, _mm_gram(ar[p][:C2], bk[p][C2:], NT), 0.0) for p in pairs]
        g_rb = [jnp.where(incl, _mm_gram(ar[p][C2:], bk[p][:C2], NT), 0.0) for p in pairs]
        g_rk = [jnp.where(incl, _mm_gram(ar[p][C2:], bk[p][C2:], NT), 0.0) for p in pairs]

    s_old = [s_scr[p] for p in pairs]
    ws = [_mm_state(ar[p] * e_mid[p], s_old[p], NT) for p in pairs]
    u = [ws[p][:C2] + _mm_apply(g_ak[p], v_st[p]) for p in pairs]
    lp = g_ab
    u = [u[p] + _mm_apply(lp[p], u[p]) for p in pairs]
    for _ in range(n_sq):
        lp = [_mm_inv(lp[p], lp[p]) for p in pairs]
        u = [u[p] + _mm_apply(lp[p], u[p]) for p in pairs]
    uv = [jnp.concatenate([u[p], v_st[p]], axis=0) for p in pairs]
    if fused:
        y_st = [ws[p][C2:] + _mm_state(jnp.concatenate([g_rb[p], g_rk[p]], axis=1), uv[p]) for p in pairs]
    else:
        y_st = [ws[p][C2:] + _mm_state(g_rb[p], u[p]) + _mm_state(g_rk[p], v_st[p]) for p in pairs]
    for p in pairs:
        s_scr[p] = s_old[p] * e_end[p] + _mm_state(uv[p], bk[p] * e_endp[p], TN)

    y = [jnp.concatenate([y_st[s * N_PAIRS + p][:C] + y_st[s * N_PAIRS + p][C:] for s in seqs], axis=0)
         for p in range(N_PAIRS)]
    for p in range(N_PAIRS):
        sl = sls[p]
        dev = y[p] - _mm1(y[p], seg_ones) * (1.0 / HEAD_DIM)
        var = _mm1(dev * dev, seg_ones) * (1.0 / HEAD_DIM)
        yn = dev * lax.rsqrt(var + LNX_EPS) * lnw_ref[:, sl] + lnb_ref[:, sl]
        gate = pre_scr[:, PRE_GATE + LANES * p:PRE_GATE + LANES * (p + 1)]
        y_ref[:, :, sl] = ((yn + bonus_scr[:, sl]) * gate).reshape(nb, C, LANES).astype(y_ref.dtype)

    @pl.when(ci == n_chunks - 1)
    def _():
        for s in seqs:
            for p in range(N_PAIRS):
                s_fin = s_scr[s * N_PAIRS + p]
                sfin_ref[s, 2 * p] = s_fin[:HEAD_DIM, :HEAD_DIM]
                sfin_ref[s, 2 * p + 1] = s_fin[HEAD_DIM:, HEAD_DIM:]


RWKV_SEQS = 8
PRE_W = 0
PRE_GATE = 2 * RWKV_W
PRE_K = 3 * RWKV_W
PRE_SQ = 4 * RWKV_W
PRE_COLS = 5 * RWKV_W


def _rwkv(proj_rw, pstart, s0, wts, C):
    b, t, _ = proj_rw.shape
    nb = RWKV_SEQS
    assert b % nb == 0 and t % C == 0
    n_chunks = t // C
    vec = lambda n: _const_spec((1, n))
    kern = functools.partial(_rwkv_kernel, C=C, n_chunks=n_chunks, nb=nb)
    return pl.pallas_call(
        kern,
        grid=(b // nb, n_chunks),
        in_specs=[pl.BlockSpec((nb, C, RWKV_COLS), lambda i, j: (i, j, 0)),
                  pl.BlockSpec((nb, 1, RWKV_COLS), lambda i, j: (i, 0, 0)),
                  pl.BlockSpec((nb, N_HEADS, HEAD_DIM, HEAD_DIM), lambda i, j: (i, 0, 0, 0)),
                  vec(RWKV_COLS), vec(RWKV_W), vec(RWKV_W),
                  _const_spec((LANES, 2 * RWKV_W)), _const_spec((LANES, 2 * RWKV_W)),
                  _const_spec((D_GATE, RWKV_W)), _const_spec((D_GATE, RWKV_W)),
                  vec(RWKV_W), vec(RWKV_W), vec(RWKV_W), vec(RWKV_W), vec(RWKV_W)],
        out_specs=[pl.BlockSpec((nb, C, RWKV_W), lambda i, j: (i, j, 0)),
                   pl.BlockSpec((nb, N_HEADS, HEAD_DIM, HEAD_DIM), lambda i, j: (i, 0, 0, 0))],
        out_shape=[jax.ShapeDtypeStruct((b, t, RWKV_W), BF16 if C % (2 * SUBLANES) == 0 else F32),
                   jax.ShapeDtypeStruct((b, N_HEADS, HEAD_DIM, HEAD_DIM), F32)],
        scratch_shapes=[pltpu.VMEM((nb * N_PAIRS, LANES, LANES), F32), pltpu.VMEM((nb, 1, RWKV_COLS), F32),
                        pltpu.VMEM((nb * C, PRE_COLS), F32), pltpu.VMEM((nb * C, RWKV_W), F32)],
        compiler_params=_params(("parallel", "arbitrary")),
        name="rwkv",
    )(proj_rw, pstart, s0, *wts)


ATT_UNROLL = 8
LOG2E = 1.4426950408889634


def _attn_prompt_kernel(q_ref, k_ref, v_ref, bias_ref, o_ref, acc_scr, m_scr, l_scr, *, T):
    QB = N_STEPS
    lane = lax.broadcasted_iota(jnp.int32, (1, LANES), 1)
    lane_lo = lane < HEAD_DIM

    def halves(x):
        return jnp.where(lane_lo, x[:QB], x[QB:])

    order = sorted(DILATIONS, reverse=True)
    assert order[-1] == 1
    for bi, d in enumerate(order):
        per_res = T // (QB * d)
        ub = min(ATT_UNROLL, per_res)
        groups = ATT_UNROLL // ub
        gpr = per_res // ub

        def rows(start, d=d):
            if d == 1:
                return pl.ds(pl.multiple_of(start, QB), QB)
            return pl.ds(start, QB, stride=d)

        def body(it, carry, bi=bi, d=d, ub=ub, groups=groups, gpr=gpr, rows=rows):
            blocks = []
            for g in range(groups):
                gi = it * groups + g
                res = gi // gpr
                n0 = (gi % gpr) * ub
                starts = [res + d * QB * jnp.maximum(n0 - 1, 0)] + [res + d * QB * (n0 + u) for u in range(ub)]
                kb = [k_ref[rows(s), :].astype(BF16) for s in starts]
                vb = [v_ref[rows(s), :].astype(BF16) for s in starts]
                for u in range(ub):
                    bias = bias_ref[jnp.where(n0 == 0, 1, 0)] if u == 0 else bias_ref[0]
                    blocks.append((starts[u + 1], jnp.concatenate([kb[u], kb[u + 1]], axis=0),
                                   jnp.concatenate([vb[u], vb[u + 1]], axis=0), bias))
            q_st = []
            for start, _, _, _ in blocks:
                q = q_ref[rows(start), :] * (HEAD_DIM ** -0.5 * LOG2E)
                q_st.append(jnp.concatenate([jnp.where(lane_lo, q, 0.0), jnp.where(lane_lo, 0.0, q)],
                                            axis=0).astype(BF16))
            s = [_mm(q_st[i], blk[1], NT) + blk[3] for i, blk in enumerate(blocks)]
            m = [jnp.max(x, axis=1, keepdims=True) for x in s]
            p = [jnp.exp2(s[i] - m[i]) for i in range(len(blocks))]
            l = [jnp.sum(x, axis=1, keepdims=True) for x in p]
            o = [halves(_mm(p[i].astype(BF16), blk[2])) for i, blk in enumerate(blocks)]
            m_b = [halves(jnp.broadcast_to(x, (2 * QB, LANES))) for x in m]
            l_b = [halves(jnp.broadcast_to(x, (2 * QB, LANES))) for x in l]
            if bi == 0:
                for i, blk in enumerate(blocks):
                    acc_scr[rows(blk[0]), :] = o[i]
                    m_scr[rows(blk[0]), :] = m_b[i]
                    l_scr[rows(blk[0]), :] = l_b[i]
            else:
                m_old = [m_scr[rows(blk[0]), :] for blk in blocks]
                acc_old = [acc_scr[rows(blk[0]), :] for blk in blocks]
                l_old = [l_scr[rows(blk[0]), :] for blk in blocks]
                for i, blk in enumerate(blocks):
                    m_new = jnp.maximum(m_old[i], m_b[i])
                    w_old = jnp.exp2(m_old[i] - m_new)
                    w_new = jnp.exp2(m_b[i] - m_new)
                    acc = acc_old[i] * w_old + o[i] * w_new
                    den = l_old[i] * w_old + l_b[i] * w_new
                    if bi < len(order) - 1:
                        acc_scr[rows(blk[0]), :] = acc
                        m_scr[rows(blk[0]), :] = m_new
                        l_scr[rows(blk[0]), :] = den
                    else:
                        o_ref[rows(blk[0]), :] = (acc / den).astype(o_ref.dtype)
            return carry

        lax.fori_loop(0, d * gpr // groups, body, 0)


def _band_bias():
    qi = jnp.arange(2 * N_STEPS)[:, None] % N_STEPS
    kj = jnp.arange(2 * N_STEPS)[None, :]
    band = (kj >= qi) & (kj <= qi + N_STEPS)
    normal = jnp.where(band, 0.0, NEG_INF).astype(F32)
    first = jnp.where(band & (kj >= N_STEPS), 0.0, NEG_INF).astype(F32)
    return jnp.stack([normal, first])


def _attn_prompt(qkv):
    b, t, _ = qkv.shape
    assert t % (N_STEPS * max(DILATIONS)) == 0
    spec = pl.BlockSpec((None, t, LANES), lambda i, p: (i, 0, p))
    slab = lambda j: pl.BlockSpec((None, t, LANES), lambda i, p, j=j: (i, 0, j * N_PAIRS + p))
    q, k, v = qkv, qkv, qkv
    return pl.pallas_call(
        functools.partial(_attn_prompt_kernel, T=t),
        grid=(b, N_PAIRS),
        in_specs=[slab(0), slab(1), slab(2), _const_spec((2, 2 * N_STEPS, 2 * N_STEPS))],
        out_specs=spec,
        out_shape=jax.ShapeDtypeStruct((b, t, ATT_W), BF16),
        scratch_shapes=[pltpu.VMEM((t, LANES), F32)] * 3,
        compiler_params=_params(("parallel", "parallel")),
        name="attn_prompt",
    )(q, k, v, _band_bias())


def _attn_sample_kernel(q_ref, kn_ref, vn_ref, kc_ref, vc_ref, cntc_ref, cntn_ref,
                        o_ref, ko_ref, vo_ref, *, ts, w_buf):
    rows = N_HEADS * ts
    ri = lax.broadcasted_iota(jnp.int32, (rows, ATT_W), 0)
    cj = lax.broadcasted_iota(jnp.int32, (rows, ATT_W), 1)
    own = (ri // ts) == (cj // HEAD_DIM)
    q_all = jnp.concatenate([q_ref[...] * (HEAD_DIM ** -0.5)] * N_HEADS, axis=0)
    q_st = jnp.where(own, q_all, 0.0).astype(BF16)
    kc = kc_ref[...]
    vc = vc_ref[...]
    kn = kn_ref[...]
    vn = vn_ref[...]
    cnt_c = cntc_ref[...]
    cnt_n = cntn_ref[...]
    s_c = jnp.where(cnt_c > 0.0, _mm(q_st, kc.astype(BF16), NT), NEG_INF)
    s_n = jnp.where(cnt_n > 0.0, _mm(q_st, kn.astype(BF16), NT), NEG_INF)
    m = jnp.maximum(jnp.max(s_c, axis=1, keepdims=True), jnp.max(s_n, axis=1, keepdims=True))
    p_c = cnt_c * jnp.exp(s_c - m)
    p_n = cnt_n * jnp.exp(s_n - m)
    den = jnp.sum(p_c, axis=1, keepdims=True) + jnp.sum(p_n, axis=1, keepdims=True)
    o_st = (_mm(p_c.astype(BF16), vc.astype(BF16)) + _mm(p_n.astype(BF16), vn.astype(BF16))) / den
    o_st = jnp.where(own, o_st, 0.0)
    out = o_st[0:ts]
    for h in range(1, N_HEADS):
        out = out + o_st[h * ts:(h + 1) * ts]
    o_ref[...] = out
    ko_ref[0:w_buf - ts, :] = kc[ts:w_buf]
    ko_ref[w_buf - ts:w_buf, :] = kn
    vo_ref[0:w_buf - ts, :] = vc[ts:w_buf]
    vo_ref[w_buf - ts:w_buf, :] = vn


def _branch_counts(ts, w_buf):
    delta = (w_buf + jnp.arange(ts)[:, None]) - jnp.arange(w_buf + ts)[None, :]
    cnt = jnp.zeros(delta.shape, F32)
    for d in DILATIONS:
        cnt = cnt + ((delta >= 0) & (delta % d == 0) & (delta // d <= N_STEPS)).astype(F32)
    cnt = jnp.tile(cnt, (N_HEADS, 1))
    return cnt[:, :w_buf], cnt[:, w_buf:]


def _attn_sample(q, kn, vn, kcache, vcache):
    b, ts, _ = q.shape
    w_buf = kcache.shape[1]
    assert w_buf >= N_STEPS * max(DILATIONS) and ts % SUBLANES == 0
    cnt_c, cnt_n = _branch_counts(ts, w_buf)
    new = pl.BlockSpec((None, ts, ATT_W), lambda i: (i, 0, 0))
    cache = pl.BlockSpec((None, w_buf, ATT_W), lambda i: (i, 0, 0))
    return pl.pallas_call(
        functools.partial(_attn_sample_kernel, ts=ts, w_buf=w_buf),
        grid=(b,),
        in_specs=[new, new, new, cache, cache,
                  _const_spec((N_HEADS * ts, w_buf)), _const_spec((N_HEADS * ts, ts))],
        out_specs=[new, cache, cache],
        out_shape=[jax.ShapeDtypeStruct((b, ts, ATT_W), F32),
                   jax.ShapeDtypeStruct((b, w_buf, ATT_W), F32),
                   jax.ShapeDtypeStruct((b, w_buf, ATT_W), F32)],
        compiler_params=_params(("parallel",)),
        name="attn_sample",
    )(q, kn, vn, kcache, vcache, cnt_c, cnt_n)


FFN_TF = 256
N_FCHUNK = D_FF // FFN_TF


def _mlp_kernel(*refs, tm, tiles_per_seq, seq_len, has_state):
    if has_state:
        (x_ref, rw_ref, att_ref, wout_ref, gmix_ref, gpre_ref, gpost_ref, wup_ref, cw_ref, cb_ref,
         wd_ref, p2_ref, y_ref, u_ref, act_scr) = refs
    else:
        (x_ref, rw_ref, att_ref, wout_ref, gmix_ref, gpre_ref, gpost_ref, wup_ref, cw_ref, cb_ref,
         wd_ref, y_ref, u_ref, act_scr, ubuf_scr, carry_scr) = refs

        @pl.when((pl.program_id(0) % tiles_per_seq) == 0)
        def _():
            carry_scr[...] = jnp.zeros_like(carry_scr)

    mix = (_mm(rw_ref[...].astype(BF16), wout_ref[:RWKV_W, :])
           + _mm(att_ref[...].astype(BF16), wout_ref[RWKV_W:, :]))
    x1 = x_ref[...] + _rms(mix, gmix_ref[...])
    hb = _rms(x1, gpre_ref[...]).astype(BF16)
    t_in = lax.broadcasted_iota(jnp.int32, (tm, 1), 0) % seq_len

    for c in range(N_FCHUNK):
        conv = []
        for idx, col0 in enumerate((c * FFN_TF, D_FF + c * FFN_TF)):
            cols = slice(col0, col0 + FFN_TF)
            u = _mm(hb, wup_ref[:, cols])
            if has_state:
                p2 = p2_ref[:, cols]
                u1 = jnp.where(t_in >= 1, pltpu.roll(u, 1, 0), pltpu.roll(p2, tm - 1, 0))
                u2 = jnp.where(t_in >= 2, pltpu.roll(u, 2, 0), p2)
                u_ref[:, cols] = u
            else:
                stage = ubuf_scr.at[c % 2, idx]
                stage[0:SUBLANES, :] = carry_scr[c, idx]
                stage[SUBLANES:, :] = u
                u1 = stage[SUBLANES - 1:SUBLANES - 1 + tm, :]
                u2 = stage[SUBLANES - 2:SUBLANES - 2 + tm, :]
                carry_scr[c, idx] = u[tm - SUBLANES:]
                u_ref[:, cols] = u[tm - SUBLANES:]
            conv.append(u2 * cw_ref[0:1, cols] + u1 * cw_ref[1:2, cols] + u * cw_ref[2:3, cols]
                        + cb_ref[:, cols])
        act_scr[:, c * FFN_TF:(c + 1) * FFN_TF] = (conv[0] * _sigmoid(conv[0]) * conv[1]).astype(BF16)

    y_ref[...] = x1 + _rms(_mm(act_scr[...], wd_ref[...]), gpost_ref[...])


def _mlp(x2d, rw, att, w, tm, seq_len, state_rows=None):
    n = x2d.shape[0]
    has_state = state_rows is not None
    tiles_per_seq = max(seq_len // tm, 1)
    n_tiles = n // tm
    row = lambda i: (i, 0)
    vec = lambda: _const_spec((1, D_MODEL))
    in_specs = [pl.BlockSpec((tm, D_MODEL), row), pl.BlockSpec((tm, RWKV_W), row),
                pl.BlockSpec((tm, ATT_W), row), _const_spec((D_MODEL, D_MODEL)), vec(), vec(), vec(),
                _const_spec((D_MODEL, 2 * D_FF)), _const_spec((CONV_W, 2 * D_FF)),
                _const_spec((1, 2 * D_FF)), _const_spec((D_FF, D_MODEL))]
    args = [x2d, rw, att, w["w_out"], w["norm_mix_post"], w["norm_ffn_pre"], w["norm_ffn_post"],
            w["w_ffn_up"], w["ffn_conv_w"], w["ffn_conv_b"], w["w_ffn_down"]]
    scratch = [pltpu.VMEM((tm, D_FF), BF16)]
    if has_state:
        assert tm % seq_len == 0 and seq_len == SUBLANES
        in_specs.append(pl.BlockSpec((tm, 2 * D_FF), row))
        args.append(state_rows)
        u_spec = pl.BlockSpec((tm, 2 * D_FF), row)
        u_shape = jax.ShapeDtypeStruct((n, 2 * D_FF), F32)
    else:
        assert seq_len % tm == 0
        scratch += [pltpu.VMEM((2, 2, tm + SUBLANES, FFN_TF), F32),
                    pltpu.VMEM((N_FCHUNK, 2, SUBLANES, FFN_TF), F32)]
        u_spec = pl.BlockSpec((None, SUBLANES, 2 * D_FF), lambda i: (i, 0, 0))
        u_shape = jax.ShapeDtypeStruct((n_tiles, SUBLANES, 2 * D_FF), F32)
    kern = functools.partial(_mlp_kernel, tm=tm, tiles_per_seq=tiles_per_seq, seq_len=seq_len,
                             has_state=has_state)
    return pl.pallas_call(
        kern,
        grid=(n_tiles,),
        in_specs=in_specs,
        out_specs=[pl.BlockSpec((tm, D_MODEL), row), u_spec],
        out_shape=[jax.ShapeDtypeStruct((n, D_MODEL), F32), u_shape],
        scratch_shapes=scratch,
        compiler_params=_params(("arbitrary",)),
        name="mlp",
    )(*args)


def _rope_tables(pos):
    half = HEAD_DIM // 2
    inv = ROPE_THETA ** (-jnp.arange(half, dtype=F32) / half)
    ang = pos.astype(F32)[:, None] * inv[None, :]
    cos, sin = jnp.cos(ang), jnp.sin(ang)
    cos_t = jnp.tile(jnp.concatenate([cos, cos], axis=1), (1, LANES // HEAD_DIM))
    sin_t = jnp.tile(jnp.concatenate([-sin, sin], axis=1), (1, LANES // HEAD_DIM))
    return cos_t, sin_t


def _layer(x, pos, h_prev, wkv0, k_buf, v_buf, conv_prev, w, tm, chunk, mlp_tm):
    b, t, _ = x.shape
    n = b * t
    x2d = x.reshape(n, D_MODEL)
    cos_t, sin_t = _rope_tables(pos)
    if t < tm:
        cos_t = jnp.tile(cos_t, (tm // t, 1))
        sin_t = jnp.tile(sin_t, (tm // t, 1))
        pos_tiles = 1
    else:
        pos_tiles = t // tm
    keep = min(N_STEPS * max(DILATIONS), t)
    tail_tiles = keep // tm if k_buf is None else 0
    rw, qkv, *tails = _in_proj(x2d, w["norm_mix_pre"], w["w_in"], cos_t, sin_t, tm, pos_tiles, tail_tiles)
    h_last, pstart = _shift_state(x[:, -1], w["norm_mix_pre"], h_prev, w["w_in_rw"])

    rwkv_out, wkv_new = _rwkv(rw.reshape(b, t, RWKV_COLS), pstart.reshape(b, 1, RWKV_COLS), wkv0,
                              w["rwkv"], chunk)
    qkv3 = qkv.reshape(b, t, 3 * ATT_W)
    if k_buf is None:
        att = _attn_prompt(qkv3)
        k_new, v_new = tails
    else:
        w_buf = k_buf.shape[1]
        q3, k3, v3 = (qkv3[:, :, ATT_W * i:ATT_W * (i + 1)] for i in range(3))
        att, k_new, v_new = _attn_sample(q3, k3, v3, k_buf.reshape(b, w_buf, ATT_W),
                                         v_buf.reshape(b, w_buf, ATT_W))
    rw2d, att2d = rwkv_out.reshape(n, RWKV_W), att.reshape(n, ATT_W)
    if conv_prev is None:
        y, tails = _mlp(x2d, rw2d, att2d, w, mlp_tm, t)
        conv_new = tails.reshape(b, t // mlp_tm, SUBLANES, 2 * D_FF)[:, -1, SUBLANES - (CONV_W - 1):]
    else:
        p2 = jnp.pad(conv_prev, ((0, 0), (0, t - (CONV_W - 1)), (0, 0))).reshape(n, 2 * D_FF)
        y, u = _mlp(x2d, rw2d, att2d, w, mlp_tm, t, state_rows=p2)
        conv_new = u.reshape(b, t, 2 * D_FF)[:, t - (CONV_W - 1):]
    heads = lambda a: a.reshape(b, a.shape[1], N_HEADS, HEAD_DIM)
    return y.reshape(b, t, D_MODEL), h_last, wkv_new, heads(k_new), heads(v_new), conv_new


def _prep_weights(norm_mix_pre, norm_mix_post, norm_ffn_pre, norm_ffn_post, w_in, mu_shift, w0,
                  w_decay_up, a0, w_iclr_up, w_gate_up, k_k, k_a, r_k, lnx_w, lnx_b, w_out,
                  w_ffn_up, ffn_conv_w, ffn_conv_b, w_ffn_down):
    vec = lambda a: a.reshape(1, -1)
    zero = jnp.zeros((D_DECAY, RWKV_W), F32)
    lora = jnp.concatenate([jnp.concatenate([w_decay_up, zero], axis=1),
                            jnp.concatenate([zero, w_iclr_up], axis=1)], axis=0)
    lora_hi, lora_lo = _split(lora)
    wg_hi, wg_lo = _split(w_gate_up)
    w_in_bf = w_in.astype(BF16)
    return {
        "norm_mix_pre": vec(norm_mix_pre), "norm_mix_post": vec(norm_mix_post),
        "norm_ffn_pre": vec(norm_ffn_pre), "norm_ffn_post": vec(norm_ffn_post),
        "w_in": w_in_bf, "w_in_rw": w_in_bf[:, :RWKV_COLS],
        "rwkv": (vec(mu_shift), vec(w0), vec(a0), lora_hi, lora_lo, wg_hi, wg_lo,
                 vec(k_k), vec(k_a), vec(r_k), vec(lnx_w), vec(lnx_b)),
        "w_out": w_out.astype(BF16), "w_ffn_up": w_ffn_up.astype(BF16),
        "ffn_conv_w": ffn_conv_w, "ffn_conv_b": vec(ffn_conv_b), "w_ffn_down": w_ffn_down.astype(BF16),
    }


PROMPT_TM = 512
IN_PROJ_TM = 256
PROMPT_CHUNK = 64
SAMPLE_MLP_TM = 128


def kernel(x_prompt, x_sample, state_rwkv_shift, state_rwkv_wkv, cache_att_k, cache_att_v, state_ffn_conv, norm_mix_pre, norm_mix_post, norm_ffn_pre, norm_ffn_post, w_in, mu_shift, w0, w_decay_up, a0, w_iclr_up, w_gate_up, k_k, k_a, r_k, lnx_w, lnx_b, w_out, w_ffn_up, ffn_conv_w, ffn_conv_b, w_ffn_down):
    bp, tp, _ = x_prompt.shape
    bs, ts, _ = x_sample.shape
    depth = norm_mix_pre.shape[0]
    pos_p = jnp.arange(tp, dtype=jnp.int32)
    pos_s = PAST_LEN + jnp.arange(ts, dtype=jnp.int32)
    yp, ys = x_prompt, x_sample
    outs_p = [[] for _ in range(5)]
    outs_s = [[] for _ in range(5)]
    for l in range(depth):
        w = _prep_weights(norm_mix_pre[l], norm_mix_post[l], norm_ffn_pre[l], norm_ffn_post[l], w_in[l],
                          mu_shift[l], w0[l], w_decay_up[l], a0[l], w_iclr_up[l], w_gate_up[l], k_k[l],
                          k_a[l], r_k[l], lnx_w[l], lnx_b[l], w_out[l], w_ffn_up[l], ffn_conv_w[l],
                          ffn_conv_b[l], w_ffn_down[l])
        yp, *state_p = _layer(yp, pos_p, jnp.zeros((bp, D_MODEL), F32),
                              jnp.zeros((bp, N_HEADS, HEAD_DIM, HEAD_DIM), F32), None, None, None,
                              w, min(IN_PROJ_TM, tp), min(PROMPT_CHUNK, tp), min(PROMPT_TM, tp))
        ys, *state_s = _layer(ys, pos_s, state_rwkv_shift[l], state_rwkv_wkv[l], cache_att_k[l],
                              cache_att_v[l], state_ffn_conv[l], w, bs * ts, ts, SAMPLE_MLP_TM)
        for acc, val in zip(outs_p, state_p):
            acc.append(val)
        for acc, val in zip(outs_s, state_s):
            acc.append(val)
    return (yp, ys, *(jnp.stack(a) for a in outs_p), *(jnp.stack(a) for a in outs_s))
```
